```python
import jax, jax.numpy as jnp
from jax import lax
import numpy as np

D_MODEL = 4096
BATCH = 4
SEQ = 2048
DEPTH = 2
DEC_BATCH = 8
DEC_SEQ = 32
PAST_LEN = 4096

CHUNK = 64
LA_BLOCK = 64
BRANCH_WIDTH = D_MODEL // 2
HG_WIDTH = BRANCH_WIDTH
HG_HEAD_DIM = 128
HG_HEADS = HG_WIDTH // HG_HEAD_DIM
LRU_WIDTH = BRANCH_WIDTH
LRU_BLOCKS = 16
LRU_BLOCK_DIM = LRU_WIDTH // LRU_BLOCKS
CONV_WIDTH = 4
LRU_C = 8.0
GLA_V_WIDTH = BRANCH_WIDTH
GLA_K_WIDTH = GLA_V_WIDTH // 2
GLA_HEADS = 4
GLA_DK = GLA_K_WIDTH // GLA_HEADS
GLA_DV = GLA_V_WIDTH // GLA_HEADS
GLA_RANK = 16
GLA_TAU = 16.0
N_BRANCH = 3
EPS = 1e-6
F_FLOOR = 1e-6
SPLIT_SIZES = (HG_WIDTH, HG_WIDTH, HG_WIDTH, HG_WIDTH,
               LRU_WIDTH, LRU_WIDTH,
               GLA_K_WIDTH, GLA_K_WIDTH, GLA_V_WIDTH, GLA_RANK, GLA_V_WIDTH,
               N_BRANCH * D_MODEL)
IN_COLS = sum(SPLIT_SIZES)
SPLIT_POINTS = tuple(sum(SPLIT_SIZES[:i + 1]) for i in range(len(SPLIT_SIZES) - 1))

kernel_name = "hybrid_hgrn2_rglru_gla_stream_step"

F32 = jnp.float32


def rms_norm(x, w):
    xf = x.astype(F32)
    y = xf * lax.rsqrt(jnp.mean(xf * xf, axis=-1, keepdims=True) + EPS)
    return (y * w.astype(F32)).astype(x.dtype)


def head_rmsnorm_gate(o, w, gate):
    B, T, H, dh = o.shape
    on = o * lax.rsqrt(jnp.mean(o * o, axis=-1, keepdims=True) + EPS)
    return on.reshape(B, T, H * dh) * w.astype(F32) * jax.nn.silu(gate.astype(F32))


def chunked_gated_linear_attention(q, k, v, log_g, s0):
    B, T, H, _ = q.shape
    dv = v.shape[-1]
    n_blk = -(-T // LA_BLOCK)
    pad = n_blk * LA_BLOCK - T

    def prep(a):
        a = jnp.pad(a.astype(F32), ((0, 0), (0, pad), (0, 0), (0, 0)))
        return a.reshape(B, n_blk, LA_BLOCK, H, a.shape[-1]).swapaxes(0, 1)

    qb, kb, vb, gb = prep(q), prep(k), prep(v), prep(log_g)
    causal = jnp.tril(jnp.ones((LA_BLOCK, LA_BLOCK), dtype=bool))[None, :, :, None, None]

    def step(s, blk):
        qc, kc, vc, gc = blk
        b = jnp.cumsum(gc, axis=1)
        diff = b[:, :, None] - b[:, None, :]
        decay = jnp.where(causal, jnp.exp(jnp.where(causal, diff, 0.0)), 0.0)
        attn = jnp.einsum('btshd,bshd->bhts', qc[:, :, None] * decay, kc)
        o = (jnp.einsum('bhts,bshv->bthv', attn, vc)
             + jnp.einsum('bthd,bhdv->bthv', qc * jnp.exp(b), s))
        b_last = b[:, -1]
        s_new = (jnp.exp(b_last)[..., None] * s
                 + jnp.einsum('bshd,bshv->bhdv', kc * jnp.exp(b_last[:, None] - b), vc))
        return s_new, o

    s_fin, ob = lax.scan(step, s0.astype(F32), (qb, kb, vb, gb))
    o = ob.swapaxes(0, 1).reshape(B, n_blk * LA_BLOCK, H, dv)[:, :T]
    return o, s_fin


def hgrn2_branch(q_in, f_in, i_in, s0, lb):
    B, T, _ = q_in.shape
    z = f_in.astype(F32)
    lbf = lb.astype(F32)
    f = lbf + (1.0 - lbf) * jax.nn.sigmoid(z)
    log_f = jnp.log(jnp.maximum(f, F_FLOOR))
    k = 1.0 - f
    q = jax.nn.silu(q_in.astype(F32))

    def heads(a):
        return a.reshape(B, T, HG_HEADS, HG_HEAD_DIM)

    return chunked_gated_linear_attention(heads(q), heads(k), heads(i_in), heads(log_f), s0)


def rglru_branch(xb, conv_state, h0, conv_w, conv_b, wa, ba, wx, bx, lam):
    B, T, W = xb.shape
    xp = jnp.concatenate([conv_state.astype(xb.dtype), xb], axis=1)
    xc = conv_b.astype(F32)
    for j in range(CONV_WIDTH):
        xc = xc + xp[:, j:j + T].astype(F32) * conv_w[j].astype(F32)
    new_conv = xp[:, T:]
    xh = xc.reshape(B, T, LRU_BLOCKS, LRU_BLOCK_DIM)
    r = jax.nn.sigmoid(jnp.einsum('bthi,hij->bthj', xh, wa.astype(F32)).reshape(B, T, W) + ba.astype(F32))
    i = jax.nn.sigmoid(jnp.einsum('bthi,hij->bthj', xh, wx.astype(F32)).reshape(B, T, W) + bx.astype(F32))
    log_a = -LRU_C * r * jax.nn.softplus(-lam.astype(F32))
    a = jnp.exp(log_a)
    u = jnp.sqrt(jnp.maximum(-jnp.expm1(2.0 * log_a), 0.0)) * (i * xc)
    u = u.at[:, 0].add(a[:, 0] * h0.astype(F32))

    def combine(left, right):
        a1, b1 = left
        a2, b2 = right
        return a1 * a2, a2 * b1 + b2

    _, h = lax.associative_scan(combine, (a, u), axis=1)
    return h, new_conv, h[:, -1]


def trunk_layer(x, hg_state, lru_h, lru_conv, gla_state, lb, norm_pre, norm_post, w_in,
                hg_norm, conv_w, conv_b, wa, ba, wx, bx, lam, gla_w2, gla_b2, gla_norm,
                w_branch, w_out):
    B, T, _ = x.shape
    z = rms_norm(x, norm_pre)
    proj = jnp.einsum('btd,dc->btc', z, w_in)
    (hq, hf, hi, hgate, lx, lgate, cq, ck, cv, clr, cgate, mgate) = jnp.split(proj, SPLIT_POINTS, axis=-1)

    oa, new_hg = hgrn2_branch(hq, hf, hi, hg_state, lb)
    ya = head_rmsnorm_gate(oa, hg_norm, hgate)

    hb, new_conv, new_h = rglru_branch(lx, lru_conv, lru_h, conv_w, conv_b, wa, ba, wx, bx, lam)
    yb = hb * jax.nn.silu(lgate.astype(F32))

    gate_logits = jnp.einsum('btr,rk->btk', clr.astype(F32), gla_w2.astype(F32)) + gla_b2.astype(F32)
    log_alpha = jax.nn.log_sigmoid(gate_logits) / GLA_TAU
    qg = cq.astype(F32).reshape(B, T, GLA_HEADS, GLA_DK) * (GLA_DK ** -0.5)
    kg = ck.reshape(B, T, GLA_HEADS, GLA_DK)
    vg = cv.reshape(B, T, GLA_HEADS, GLA_DV)
    oc, new_gla = chunked_gated_linear_attention(qg, kg, vg, log_alpha.reshape(B, T, GLA_HEADS, GLA_DK), gla_state)
    yc = head_rmsnorm_gate(oc, gla_norm, cgate)

    mg = mgate.reshape(B, T, N_BRANCH, D_MODEL)
    merged = jnp.zeros((B, T, D_MODEL), F32)
    for n, yn in enumerate((ya, yb, yc)):
        un = jnp.einsum('btc,cd->btd', yn.astype(x.dtype), w_branch[n]).astype(F32)
        merged = merged + jax.nn.sigmoid(mg[:, :, n].astype(F32)) * un
    out = jnp.einsum('btd,de->bte', merged.astype(x.dtype), w_out)
    y = x + rms_norm(out, norm_post)
    return (y, new_hg.astype(hg_state.dtype), new_h.astype(lru_h.dtype),
            new_conv.astype(lru_conv.dtype), new_gla.astype(gla_state.dtype))


def setup_inputs(seed: int = 0) -> dict:
    key = jax.random.key(seed)
    ks = jax.random.split(key, 24)
    nrm = jax.random.normal
    d = D_MODEL
    a_target = jax.random.uniform(ks[17], (DEPTH, LRU_WIDTH), F32, 0.9, 0.999)
    s = a_target ** (1.0 / LRU_C)
    lru_lambda = jnp.log(s) - jnp.log1p(-s)
    return {
        "x_prompt": nrm(ks[0], (BATCH, SEQ, d), F32),
        "x_sample": nrm(ks[1], (DEC_BATCH, DEC_SEQ, d), F32),
        "state_hgrn": 0.5 * nrm(ks[2], (DEPTH, DEC_BATCH, HG_HEADS, HG_HEAD_DIM, HG_HEAD_DIM), F32),
        "state_lru_h": 0.5 * nrm(ks[3], (DEPTH, DEC_BATCH, LRU_WIDTH), F32),
        "state_lru_conv": nrm(ks[4], (DEPTH, DEC_BATCH, CONV_WIDTH - 1, LRU_WIDTH), F32),
        "state_gla": nrm(ks[5], (DEPTH, DEC_BATCH, GLA_HEADS, GLA_DK, GLA_DV), F32),
        "norm_pre": 1.0 + 0.02 * nrm(ks[6], (DEPTH, d), F32),
        "norm_post": 1.0 + 0.02 * nrm(ks[7], (DEPTH, d), F32),
        "w_in": nrm(ks[8], (DEPTH, d, IN_COLS), F32) * d ** -0.5,
        "hg_lb_logits": 0.1 * nrm(ks[9], (DEPTH, HG_WIDTH), F32),
        "hg_norm": 1.0 + 0.02 * nrm(ks[10], (DEPTH, HG_WIDTH), F32),
        "lru_conv_w": nrm(ks[11], (DEPTH, CONV_WIDTH, LRU_WIDTH), F32) * CONV_WIDTH ** -0.5,
        "lru_conv_b": 0.01 * nrm(ks[12], (DEPTH, LRU_WIDTH), F32),
        "lru_wa": nrm(ks[13], (DEPTH, LRU_BLOCKS, LRU_BLOCK_DIM, LRU_BLOCK_DIM), F32) * LRU_BLOCK_DIM ** -0.5,
        "lru_ba": 0.01 * nrm(ks[14], (DEPTH, LRU_WIDTH), F32),
        "lru_wx": nrm(ks[15], (DEPTH, LRU_BLOCKS, LRU_BLOCK_DIM, LRU_BLOCK_DIM), F32) * LRU_BLOCK_DIM ** -0.5,
        "lru_bx": 0.01 * nrm(ks[16], (DEPTH, LRU_WIDTH), F32),
        "lru_lambda": lru_lambda,
        "gla_w2": nrm(ks[18], (DEPTH, GLA_RANK, GLA_K_WIDTH), F32) * GLA_RANK ** -0.5,
        "gla_b2": 0.01 * nrm(ks[19], (DEPTH, GLA_K_WIDTH), F32),
        "gla_norm": 1.0 + 0.02 * nrm(ks[20], (DEPTH, GLA_V_WIDTH), F32),
        "w_branch": nrm(ks[21], (DEPTH, N_BRANCH, BRANCH_WIDTH, d), F32) * BRANCH_WIDTH ** -0.5,
        "w_out": nrm(ks[22], (DEPTH, d, d), F32) * d ** -0.5,
    }


def reference(x_prompt, x_sample, state_hgrn, state_lru_h, state_lru_conv, state_gla,
              norm_pre, norm_post, w_in, hg_lb_logits, hg_norm, lru_conv_w, lru_conv_b,
              lru_wa, lru_ba, lru_wx, lru_bx, lru_lambda, gla_w2, gla_b2, gla_norm,
              w_branch, w_out):
    lb_sm = jax.nn.softmax(hg_lb_logits.astype(F32), axis=0)
    lb_all = jnp.cumsum(lb_sm, axis=0) - lb_sm[0:1]

    dt = x_prompt.dtype
    hp = jnp.zeros((BATCH, HG_HEADS, HG_HEAD_DIM, HG_HEAD_DIM), dt)
    lhp = jnp.zeros((BATCH, LRU_WIDTH), dt)
    lcp = jnp.zeros((BATCH, CONV_WIDTH - 1, LRU_WIDTH), dt)
    gp = jnp.zeros((BATCH, GLA_HEADS, GLA_DK, GLA_DV), dt)

    yp, ys = x_prompt, x_sample
    hg_p, hg_s, lh_p, lh_s, lc_p, lc_s, gl_p, gl_s = [], [], [], [], [], [], [], []
    for l in range(DEPTH):
        lp = (lb_all[l], norm_pre[l], norm_post[l], w_in[l], hg_norm[l], lru_conv_w[l],
              lru_conv_b[l], lru_wa[l], lru_ba[l], lru_wx[l], lru_bx[l], lru_lambda[l],
              gla_w2[l], gla_b2[l], gla_norm[l], w_branch[l], w_out[l])
        yp, a1, a2, a3, a4 = trunk_layer(yp, hp, lhp, lcp, gp, *lp)
        hg_p.append(a1); lh_p.append(a2); lc_p.append(a3); gl_p.append(a4)
        ys, b1, b2, b3, b4 = trunk_layer(ys, state_hgrn[l], state_lru_h[l], state_lru_conv[l], state_gla[l], *lp)
        hg_s.append(b1); lh_s.append(b2); lc_s.append(b3); gl_s.append(b4)

    new_hgrn_p = jnp.stack(hg_p)
    new_hgrn_s = jnp.stack(hg_s)
    new_lru_h_p = jnp.stack(lh_p)
    new_lru_h_s = jnp.stack(lh_s)
    new_lru_conv_p = jnp.stack(lc_p)
    new_lru_conv_s = jnp.stack(lc_s)
    new_gla_p = jnp.stack(gl_p)
    new_gla_s = jnp.stack(gl_s)
    return (yp, ys, new_hgrn_p, new_hgrn_s, new_lru_h_p, new_lru_h_s,
            new_lru_conv_p, new_lru_conv_s, new_gla_p, new_gla_s)
```

```python
import functools

import jax
import jax.numpy as jnp
from jax import lax
from jax.experimental import pallas as pl
from jax.experimental.pallas import tpu as pltpu

F32 = jnp.float32
BF16 = jnp.bfloat16

EPS = 1e-6
F_FLOOR = 1e-6
LRU_C = 8.0
GLA_TAU = 16.0
NEG_BIG = -1e30
SUBLANES = 8
LANES = 128
LA_CHUNK = 64
V7X_VMEM_LIMIT = 60000 * 1024


def _vmem_limit(nbytes):
    return int(min(V7X_VMEM_LIMIT, max(32 * 1024 * 1024, nbytes * 5 // 4 + (4 << 20))))


def _pick(n, candidates):
    for c in candidates:
        if n % c == 0:
            return c
    return n


def _dot(a, b):
    return jnp.dot(a, b, preferred_element_type=F32)


def _dot_nt(a, b):
    return lax.dot_general(a, b, (((1,), (1,)), ((), ())), preferred_element_type=F32)


def _dot_tn(a, b):
    return lax.dot_general(a, b, (((0,), (0,)), ((), ())), preferred_element_type=F32)


def _sigmoid(x):
    return jax.nn.sigmoid(x)


def _silu(x):
    return x * jax.nn.sigmoid(x)


def _prenorm_kernel(x_ref, w_ref, z_ref):
    x = x_ref[...]
    ms = jnp.mean(x * x, axis=-1, keepdims=True)
    z_ref[...] = (x * lax.rsqrt(ms + EPS) * w_ref[...]).astype(z_ref.dtype)


def _prenorm(x, w):
    m, d = x.shape
    tr = _pick(m, (264, 256, 128, 64, 32, 16))
    return pl.pallas_call(
        _prenorm_kernel,
        grid=(m // tr,),
        in_specs=[pl.BlockSpec((tr, d), lambda i: (i, 0)),
                  pl.BlockSpec((1, d), lambda i: (0, 0))],
        out_specs=pl.BlockSpec((tr, d), lambda i: (i, 0)),
        out_shape=jax.ShapeDtypeStruct((m, d), BF16),
        compiler_params=pltpu.CompilerParams(dimension_semantics=("arbitrary",)),
        name="prenorm",
    )(x, w.reshape(1, d))


def _postnorm_kernel(x_ref, o_ref, w_ref, y_ref):
    o = o_ref[...]
    ms = jnp.mean(o * o, axis=-1, keepdims=True)
    y_ref[...] = x_ref[...] + o * lax.rsqrt(ms + EPS) * w_ref[...]


def _postnorm(x, out, w):
    m, d = x.shape
    tr = _pick(m, (264, 256, 128, 64, 32, 16))
    return pl.pallas_call(
        _postnorm_kernel,
        grid=(m // tr,),
        in_specs=[pl.BlockSpec((tr, d), lambda i: (i, 0)),
                  pl.BlockSpec((tr, d), lambda i: (i, 0)),
                  pl.BlockSpec((1, d), lambda i: (0, 0))],
        out_specs=pl.BlockSpec((tr, d), lambda i: (i, 0)),
        out_shape=jax.ShapeDtypeStruct((m, d), F32),
        compiler_params=pltpu.CompilerParams(dimension_semantics=("arbitrary",)),
        name="postnorm",
    )(x, out, w.reshape(1, d))


def _mm_kernel(a_ref, w_ref, o_ref):
    o_ref[...] = _dot(a_ref[...], w_ref[...]).astype(o_ref.dtype)


def _matmul(a, w, out_dtype=F32, name="matmul"):
    m, k = a.shape
    n = w.shape[1]
    tm = _pick(m, (1056, 1024, 512, 320, 256, 128, 64, 32, 16))
    tn = _pick(n, (1024, 512, 256, 128))
    osz = jnp.dtype(out_dtype).itemsize
    need = 2 * (tm * k * 2 + k * tn * 2 + tm * tn * osz)
    return pl.pallas_call(
        _mm_kernel,
        grid=(m // tm, n // tn),
        in_specs=[pl.BlockSpec((tm, k), lambda i, j: (i, 0)),
                  pl.BlockSpec((k, tn), lambda i, j: (0, j))],
        out_specs=pl.BlockSpec((tm, tn), lambda i, j: (i, j)),
        out_shape=jax.ShapeDtypeStruct((m, n), out_dtype),
        compiler_params=pltpu.CompilerParams(
            dimension_semantics=("arbitrary", "arbitrary"),
            vmem_limit_bytes=_vmem_limit(need)),
        name=name,
    )(a, w)


def _tile_row_index(n, d):
    return lax.broadcasted_iota(jnp.int32, (n, d), 0) & (SUBLANES - 1)


def _tile_cumsum(g, rowi):
    c = g
    s = 1
    while s < SUBLANES:
        c = c + jnp.where(rowi >= s, pltpu.roll(c, s, 0), 0.0)
        s *= 2
    return c


def _gla_chunk(q, k, v, g, st):
    L, dk = q.shape
    nt = L // SUBLANES
    rowk = _tile_row_index(L, dk)

    c = _tile_cumsum(g, rowk)
    tot = [c[SUBLANES * i + SUBLANES - 1:SUBLANES * (i + 1), :] for i in range(nt)]
    totb = jnp.concatenate([jnp.broadcast_to(t, (SUBLANES, dk)) for t in tot], axis=0)
    qt = q * jnp.exp(c)
    knew = k * jnp.exp(totb - c)

    o = jnp.sum(q * k, axis=1, keepdims=True) * v
    for d in range(1, SUBLANES):
        kr = pltpu.roll(k, d, 0)
        cr = pltpu.roll(c, d, 0)
        vr = pltpu.roll(v, d, 0)
        e = jnp.exp(jnp.where(rowk >= d, c - cr, NEG_BIG))
        a = jnp.sum(q * kr * e, axis=1, keepdims=True)
        o = o + a * vr

    tiles = []
    arows = [jnp.zeros((SUBLANES, L), F32)]
    qhat = [qt[0:SUBLANES]]
    er = None
    for i in range(nt):
        lo, hi = SUBLANES * i, SUBLANES * (i + 1)
        if i > 0:
            pad = jnp.zeros((L - lo, dk), F32)
            kh = jnp.concatenate(tiles + [pad], axis=0).astype(BF16)
            arows.append(_dot_nt(qt[lo:hi].astype(BF16), kh))
            qhat.append(qt[lo:hi] * er)
        dcy = jnp.exp(tot[i])
        tiles = [t * dcy for t in tiles] + [knew[lo:hi]]
        er = dcy if er is None else er * dcy

    khat = jnp.concatenate(tiles, axis=0).astype(BF16)
    qh = jnp.concatenate(qhat, axis=0).astype(BF16)
    a_off = jnp.concatenate(arows, axis=0).astype(BF16)
    vb = v.astype(BF16)
    o = o + _dot(a_off, vb) + _dot_nt(qh, st.astype(BF16))
    st_new = st * er + _dot_tn(vb, khat)
    return o, st_new


def _head_norm_gate(o, w, gate):
    on = o * lax.rsqrt(jnp.mean(o * o, axis=-1, keepdims=True) + EPS)
    return on * w * _silu(gate)


def _hgrn_kernel(q_ref, f_ref, i_ref, gate_ref, lb_ref, nw_ref, s0_ref,
                 y_ref, sout_ref, st_ref, *, chunk):
    t = pl.program_id(2)
    tb = q_ref.shape[0]

    @pl.when(t == 0)
    def _():
        st_ref[...] = s0_ref[...].T

    lb = lb_ref[...]
    nw = nw_ref[...]

    def body(ci, carry):
        r0 = pl.multiple_of(ci * chunk, chunk)
        rows = pl.ds(r0, chunk)
        f = lb + (1.0 - lb) * _sigmoid(f_ref[rows, :])
        g = jnp.log(jnp.maximum(f, F_FLOOR))
        o, st_new = _gla_chunk(_silu(q_ref[rows, :]), 1.0 - f, i_ref[rows, :], g, st_ref[...])
        st_ref[...] = st_new
        y_ref[rows, :] = _head_norm_gate(o, nw, gate_ref[rows, :]).astype(y_ref.dtype)
        return carry

    lax.fori_loop(0, tb // chunk, body, 0)

    @pl.when(t == pl.num_programs(2) - 1)
    def _():
        sout_ref[...] = st_ref[...].T


def _hgrn(proj, row0, nb, t_len, lb, hg_norm, s0, col0):
    _, nh, dk, dv = s0.shape
    w = nh * dk
    tb = _pick(t_len, (512, 256, 128, 64, 32))
    chunk = min(LA_CHUNK, tb)
    nt = t_len // tb
    rb0 = row0 // tb
    cb = col0 // dk
    wb = w // dk

    def col(g):
        return pl.BlockSpec((tb, dk), lambda b, h, t: (rb0 + b * nt + t, cb + g * wb + h))

    vec = pl.BlockSpec((1, dk), lambda b, h, t: (0, h))
    st = pl.BlockSpec((None, None, dk, dv), lambda b, h, t: (b, h, 0, 0))
    y, s_new = pl.pallas_call(
        functools.partial(_hgrn_kernel, chunk=chunk),
        grid=(nb, nh, nt),
        in_specs=[col(0), col(1), col(2), col(3), vec, vec, st],
        out_specs=[pl.BlockSpec((tb, dv), lambda b, h, t: (b * nt + t, h)), st],
        out_shape=[jax.ShapeDtypeStruct((nb * t_len, w), BF16),
                   jax.ShapeDtypeStruct(s0.shape, F32)],
        scratch_shapes=[pltpu.VMEM((dv, dk), F32)],
        compiler_params=pltpu.CompilerParams(
            dimension_semantics=("arbitrary", "arbitrary", "arbitrary")),
        name="hgrn2",
    )(proj, proj, proj, proj, lb.reshape(1, w), hg_norm.reshape(1, w), s0)
    return y, s_new


def _gla_kernel(q_ref, k_ref, v_ref, gate_ref, lr_ref, w2_ref, b2_ref, nw_ref, s0_ref,
                y_ref, sout_ref, st_ref, *, chunk, scale):
    t = pl.program_id(2)
    tb = q_ref.shape[0]

    @pl.when(t == 0)
    def _():
        st_ref[...] = s0_ref[...].T

    w2 = w2_ref[...].astype(BF16)
    b2 = b2_ref[...]
    nw = nw_ref[...]

    def body(ci, carry):
        r0 = pl.multiple_of(ci * chunk, chunk)
        rows = pl.ds(r0, chunk)
        logits = _dot(lr_ref[rows, :].astype(BF16), w2) + b2
        g = (jnp.minimum(logits, 0.0) - jnp.log1p(jnp.exp(-jnp.abs(logits)))) / GLA_TAU
        o, st_new = _gla_chunk(q_ref[rows, :] * scale, k_ref[rows, :], v_ref[rows, :], g, st_ref[...])
        st_ref[...] = st_new
        y_ref[rows, :] = _head_norm_gate(o, nw, gate_ref[rows, :]).astype(y_ref.dtype)
        return carry

    lax.fori_loop(0, tb // chunk, body, 0)

    @pl.when(t == pl.num_programs(2) - 1)
    def _():
        sout_ref[...] = st_ref[...].T


def _gla(proj, lr, row0, nb, t_len, w2p, b2, gla_norm, s0, col_q, col_k, col_v, col_g):
    _, nh, dk, dv = s0.shape
    tb = _pick(t_len, (256, 128, 64, 32))
    chunk = min(LA_CHUNK, tb)
    nt = t_len // tb
    rb0 = row0 // tb
    lrw = lr.shape[1]

    def col(c0, wd):
        return pl.BlockSpec((tb, wd), lambda b, h, t: (rb0 + b * nt + t, c0 // wd + h))

    st = pl.BlockSpec((None, None, dk, dv), lambda b, h, t: (b, h, 0, 0))
    y, s_new = pl.pallas_call(
        functools.partial(_gla_kernel, chunk=chunk, scale=float(dk) ** -0.5),
        grid=(nb, nh, nt),
        in_specs=[col(col_q, dk), col(col_k, dk), col(col_v, dv), col(col_g, dv),
                  pl.BlockSpec((tb, lrw), lambda b, h, t: (rb0 + b * nt + t, 0)),
                  pl.BlockSpec((lrw, dk), lambda b, h, t: (0, h)),
                  pl.BlockSpec((1, dk), lambda b, h, t: (0, h)),
                  pl.BlockSpec((1, dv), lambda b, h, t: (0, h)),
                  st],
        out_specs=[pl.BlockSpec((tb, dv), lambda b, h, t: (b * nt + t, h)), st],
        out_shape=[jax.ShapeDtypeStruct((nb * t_len, nh * dv), BF16),
                   jax.ShapeDtypeStruct(s0.shape, F32)],
        scratch_shapes=[pltpu.VMEM((dv, dk), F32)],
        compiler_params=pltpu.CompilerParams(
            dimension_semantics=("arbitrary", "arbitrary", "arbitrary")),
        name="gla",
    )(proj, proj, proj, proj, lr, w2p, b2.reshape(1, nh * dk), gla_norm.reshape(1, nh * dv), s0)
    return y, s_new


def _lru_kernel(x_ref, gate_ref, cs_ref, h0_ref, cw_ref, cb_ref, wa_ref, ba_ref, wx_ref, bx_ref,
                lam_ref, y_ref, hout_ref, cout_ref, hist_ref, hcar_ref, a_ref, u_ref, h_ref):
    t = pl.program_id(2)
    tb, wd = x_ref.shape
    ncw = cw_ref.shape[0]

    @pl.when(t == 0)
    def _():
        hist_ref[...] = jnp.zeros_like(hist_ref)
        hist_ref[SUBLANES - (ncw - 1):SUBLANES, :] = cs_ref[...]
        hcar_ref[...] = jnp.broadcast_to(h0_ref[...], (SUBLANES, wd))

    x = x_ref[...]
    xe = jnp.concatenate([hist_ref[...], x], axis=0)
    cw = cw_ref[...]
    xc = cb_ref[...] + x * cw[ncw - 1:ncw, :]
    for s in range(1, ncw):
        xc = xc + pltpu.roll(xe, s, 0)[SUBLANES:, :] * cw[ncw - 1 - s:ncw - s, :]
    hist_ref[...] = x[tb - SUBLANES:tb, :]

    xb = xc.astype(BF16)
    r = _sigmoid(_dot(xb, wa_ref[...].astype(BF16)) + ba_ref[...])
    gi = _sigmoid(_dot(xb, wx_ref[...].astype(BF16)) + bx_ref[...])
    lam = lam_ref[...]
    softplus_neg = jnp.maximum(-lam, 0.0) + jnp.log1p(jnp.exp(-jnp.abs(lam)))
    log_a = -LRU_C * r * softplus_neg
    a = jnp.exp(log_a)
    x2 = 2.0 * log_a
    e2 = jnp.exp(x2)
    small = jnp.where(e2 == 1.0, -x2, (1.0 - e2) * x2 / jnp.log(jnp.where(e2 > 0.25, e2, 0.5)))
    one_m_a2 = jnp.where(x2 < -0.5, 1.0 - e2, small)
    u = jnp.sqrt(jnp.maximum(one_m_a2, 0.0)) * (gi * xc)

    rowi = _tile_row_index(tb, wd)
    s = 1
    while s < SUBLANES:
        ok = rowi >= s
        u = u + jnp.where(ok, a * pltpu.roll(u, s, 0), 0.0)
        a = a * jnp.where(ok, pltpu.roll(a, s, 0), 1.0)
        s *= 2
    a_ref[...] = a
    u_ref[...] = u

    def body(i, hp):
        rows = pl.ds(pl.multiple_of(i * SUBLANES, SUBLANES), SUBLANES)
        h = u_ref[rows, :] + a_ref[rows, :] * hp
        h_ref[rows, :] = h
        return jnp.broadcast_to(h[SUBLANES - 1:SUBLANES, :], (SUBLANES, wd))

    hp = lax.fori_loop(0, tb // SUBLANES, body, hcar_ref[...])
    hcar_ref[...] = hp
    y_ref[...] = (h_ref[...] * _silu(gate_ref[...])).astype(y_ref.dtype)

    @pl.when(t == pl.num_programs(2) - 1)
    def _():
        hout_ref[...] = hp[0:1, :]
        cout_ref[...] = x[tb - (ncw - 1):tb, :]


def _lru(proj, row0, nb, t_len, conv_state, h0, conv_w, conv_b, wa, ba, wx, bx, lam, col_x, col_g):
    nblk, wd, _ = wa.shape
    w = nblk * wd
    ncw = conv_w.shape[0]
    tb = _pick(t_len, (512, 256, 128, 64, 32))
    nt = t_len // tb
    rb0 = row0 // tb

    def col(c0):
        return pl.BlockSpec((tb, wd), lambda b, h, t: (rb0 + b * nt + t, c0 // wd + h))

    vec = pl.BlockSpec((1, wd), lambda b, h, t: (0, h))
    blk = pl.BlockSpec((None, wd, wd), lambda b, h, t: (h, 0, 0))
    hspec = pl.BlockSpec((None, 1, wd), lambda b, h, t: (b, 0, h))
    cspec = pl.BlockSpec((None, ncw - 1, wd), lambda b, h, t: (b, 0, h))
    y, h_new, c_new = pl.pallas_call(
        _lru_kernel,
        grid=(nb, nblk, nt),
        in_specs=[col(col_x), col(col_g), cspec, hspec,
                  pl.BlockSpec((ncw, wd), lambda b, h, t: (0, h)), vec,
                  blk, vec, blk, vec, vec],
        out_specs=[pl.BlockSpec((tb, wd), lambda b, h, t: (b * nt + t, h)), hspec, cspec],
        out_shape=[jax.ShapeDtypeStruct((nb * t_len, w), BF16),
                   jax.ShapeDtypeStruct((nb, 1, w), F32),
                   jax.ShapeDtypeStruct((nb, ncw - 1, w), F32)],
        scratch_shapes=[pltpu.VMEM((SUBLANES, wd), F32), pltpu.VMEM((SUBLANES, wd), F32),
                        pltpu.VMEM((tb, wd), F32), pltpu.VMEM((tb, wd), F32),
                        pltpu.VMEM((tb, wd), F32)],
        compiler_params=pltpu.CompilerParams(
            dimension_semantics=("arbitrary", "arbitrary", "arbitrary")),
        name="rglru",
    )(proj, proj, conv_state, h0.reshape(nb, 1, w), conv_w, conv_b.reshape(1, w),
      wa, ba.reshape(1, w), wx, bx.reshape(1, w), lam.reshape(1, w))
    return y, h_new.reshape(nb, w), c_new


def _merge_kernel(ya_ref, yb_ref, yc_ref, w_ref, m0_ref, m1_ref, m2_ref, o_ref):
    acc = _sigmoid(m0_ref[...]) * _dot(ya_ref[...], w_ref[0])
    acc = acc + _sigmoid(m1_ref[...]) * _dot(yb_ref[...], w_ref[1])
    acc = acc + _sigmoid(m2_ref[...]) * _dot(yc_ref[...], w_ref[2])
    o_ref[...] = acc.astype(o_ref.dtype)


def _merge(ya, yb, yc, wbr, proj, col_m):
    m, bw = ya.shape
    nbr, _, d = wbr.shape
    tm = _pick(m, (528, 512, 320, 256, 128, 64, 32, 16))
    tn = next(c for c in (512, 256, 128) if d % c == 0 and col_m % c == 0)
    yspec = pl.BlockSpec((tm, bw), lambda i, j: (i, 0))

    def mg(n):
        return pl.BlockSpec((tm, tn), lambda i, j: (i, (col_m + n * d) // tn + j))

    need = 2 * (3 * tm * bw * 2 + nbr * bw * tn * 2 + 3 * tm * tn * 4 + tm * tn * 2)
    return pl.pallas_call(
        _merge_kernel,
        grid=(m // tm, d // tn),
        in_specs=[yspec, yspec, yspec,
                  pl.BlockSpec((nbr, bw, tn), lambda i, j: (0, 0, j)),
                  mg(0), mg(1), mg(2)],
        out_specs=pl.BlockSpec((tm, tn), lambda i, j: (i, j)),
        out_shape=jax.ShapeDtypeStruct((m, d), BF16),
        compiler_params=pltpu.CompilerParams(
            dimension_semantics=("arbitrary", "arbitrary"),
            vmem_limit_bytes=_vmem_limit(need)),
        name="merge",
    )(ya, yb, yc, wbr, proj, proj, proj)


def kernel(x_prompt, x_sample, state_hgrn, state_lru_h, state_lru_conv, state_gla, norm_pre, norm_post, w_in, hg_lb_logits, hg_norm, lru_conv_w, lru_conv_b, lru_wa, lru_ba, lru_wx, lru_bx, lru_lambda, gla_w2, gla_b2, gla_norm, w_branch, w_out):
    bp, tp, d = x_prompt.shape
    bs, ts, _ = x_sample.shape
    depth = w_in.shape[0]
    _, _, hg_h, hg_dk, hg_dv = state_hgrn.shape
    hg_w = hg_h * hg_dk
    lru_w = state_lru_h.shape[-1]
    _, _, gl_h, gl_dk, gl_dv = state_gla.shape
    gl_kw, gl_vw = gl_h * gl_dk, gl_h * gl_dv
    rank = gla_w2.shape[1]
    dt = x_prompt.dtype

    c_hg = 0
    c_lx = 4 * hg_w
    c_lg = c_lx + lru_w
    c_q = c_lg + lru_w
    c_k = c_q + gl_kw
    c_v = c_k + gl_kw
    c_lr = c_v + gl_vw
    c_cg = c_lr
    c_m = c_cg + gl_vw

    lb_sm = jax.nn.softmax(hg_lb_logits.astype(F32), axis=0)
    lb_all = jnp.cumsum(lb_sm, axis=0) - lb_sm[0:1]

    w_main = jnp.concatenate([w_in[:, :, :c_lr], w_in[:, :, c_lr + rank:]], axis=2).astype(BF16)
    w_lr = jnp.pad(w_in[:, :, c_lr:c_lr + rank], ((0, 0), (0, 0), (0, LANES - rank))).astype(BF16)
    w2p = jnp.pad(gla_w2, ((0, 0), (0, LANES - rank), (0, 0)))
    w_br = w_branch.astype(BF16)
    w_o = w_out.astype(BF16)

    mp, ms = bp * tp, bs * ts
    x = jnp.concatenate([x_prompt.reshape(mp, d), x_sample.reshape(ms, d)], axis=0)

    zeros = lambda shape: jnp.zeros(shape, dt)
    outs = {k: [] for k in ("hg_p", "hg_s", "lh_p", "lh_s", "lc_p", "lc_s", "gl_p", "gl_s")}
    for l in range(depth):
        z = _prenorm(x, norm_pre[l])
        proj = _matmul(z, w_main[l], name="in_proj")
        lr = _matmul(z, w_lr[l], name="lr_proj")

        groups = (
            ("p", 0, bp, tp, zeros((bp,) + state_hgrn.shape[2:]), zeros((bp, lru_w)),
             zeros((bp,) + state_lru_conv.shape[2:]), zeros((bp,) + state_gla.shape[2:])),
            ("s", mp, bs, ts, state_hgrn[l], state_lru_h[l], state_lru_conv[l], state_gla[l]),
        )
        ya, yb, yc = [], [], []
        for tag, row0, nb, t_len, s_hg, s_lh, s_lc, s_gl in groups:
            y, s = _hgrn(proj, row0, nb, t_len, lb_all[l], hg_norm[l], s_hg, c_hg)
            ya.append(y); outs["hg_" + tag].append(s)
            y, hn, cn = _lru(proj, row0, nb, t_len, s_lc, s_lh, lru_conv_w[l], lru_conv_b[l],
                             lru_wa[l], lru_ba[l], lru_wx[l], lru_bx[l], lru_lambda[l], c_lx, c_lg)
            yb.append(y); outs["lh_" + tag].append(hn); outs["lc_" + tag].append(cn)
            y, s = _gla(proj, lr, row0, nb, t_len, w2p[l], gla_b2[l], gla_norm[l], s_gl,
                        c_q, c_k, c_v, c_cg)
            yc.append(y); outs["gl_" + tag].append(s)

        merged = _merge(jnp.concatenate(ya, axis=0), jnp.concatenate(yb, axis=0),
                        jnp.concatenate(yc, axis=0), w_br[l], proj, c_m)
        out = _matmul(merged, w_o[l], name="out_proj")
        x = _postnorm(x, out, norm_post[l])

    yp = x[:mp].reshape(bp, tp, d)
    ys = x[mp:].reshape(bs, ts, d)
    st = lambda k: jnp.stack(outs[k])
    return (yp, ys, st("hg_p"), st("hg_s"), st("lh_p"), st("lh_s"),
            st("lc_p"), st("lc_s"), st("gl_p"), st("gl_s"))
```

```python
import functools

import jax
import jax.numpy as jnp
from jax import lax
from jax.experimental import pallas as pl
from jax.experimental.pallas import tpu as pltpu

F32 = jnp.float32
BF16 = jnp.bfloat16

EPS = 1e-6
F_FLOOR = 1e-6
LRU_C = 8.0
GLA_TAU = 16.0
SUBLANES = 8
LANES = 128
LA_CHUNK = 64
V7X_VMEM_LIMIT = 60000 * 1024


def _vmem_limit(nbytes):
    return int(min(V7X_VMEM_LIMIT, max(32 * 1024 * 1024, nbytes * 5 // 4 + (4 << 20))))


def _pick(n, candidates):
    for c in candidates:
        if n % c == 0:
            return c
    return n


def _dot(a, b):
    return jnp.dot(a, b, preferred_element_type=F32)


def _dot_nt(a, b):
    return lax.dot_general(a, b, (((1,), (1,)), ((), ())), preferred_element_type=F32)


def _dot_tn(a, b):
    return lax.dot_general(a, b, (((0,), (0,)), ((), ())), preferred_element_type=F32)


def _sigmoid(x):
    return jax.nn.sigmoid(x)


def _silu(x):
    return x * jax.nn.sigmoid(x)


def _shifted(a, b, shift):
    return jnp.concatenate([a, b], axis=1)[:, shift:shift + a.shape[1]]


_ARB2 = ("arbitrary", "arbitrary")
_ARB3 = ("arbitrary", "arbitrary", "arbitrary")


def _prenorm_kernel(x_ref, w_ref, z_ref):
    x = x_ref[...]
    ms = jnp.mean(x * x, axis=-1, keepdims=True)
    z_ref[...] = (x * lax.rsqrt(ms + EPS) * w_ref[...]).astype(z_ref.dtype)


def _prenorm(x, w):
    m, d = x.shape
    tr = _pick(m, (264, 256, 128, 64, 32, 16))
    return pl.pallas_call(
        _prenorm_kernel,
        grid=(m // tr,),
        in_specs=[pl.BlockSpec((tr, d), lambda i: (i, 0)),
                  pl.BlockSpec((1, d), lambda i: (0, 0))],
        out_specs=pl.BlockSpec((tr, d), lambda i: (i, 0)),
        out_shape=jax.ShapeDtypeStruct((m, d), BF16),
        compiler_params=pltpu.CompilerParams(dimension_semantics=("arbitrary",)),
        name="prenorm",
    )(x, w.reshape(1, d))


def _postnorm_kernel(x_ref, o_ref, w_ref, y_ref):
    o = o_ref[...]
    ms = jnp.mean(o * o, axis=-1, keepdims=True)
    y_ref[...] = x_ref[...] + o * lax.rsqrt(ms + EPS) * w_ref[...]


def _postnorm(x, out, w):
    m, d = x.shape
    tr = _pick(m, (264, 256, 128, 64, 32, 16))
    return pl.pallas_call(
        _postnorm_kernel,
        grid=(m // tr,),
        in_specs=[pl.BlockSpec((tr, d), lambda i: (i, 0)),
                  pl.BlockSpec((tr, d), lambda i: (i, 0)),
                  pl.BlockSpec((1, d), lambda i: (0, 0))],
        out_specs=pl.BlockSpec((tr, d), lambda i: (i, 0)),
        out_shape=jax.ShapeDtypeStruct((m, d), F32),
        compiler_params=pltpu.CompilerParams(dimension_semantics=("arbitrary",)),
        name="postnorm",
    )(x, out, w.reshape(1, d))


def _mm_kernel(a_ref, w_ref, o_ref):
    o_ref[...] = _dot(a_ref[...], w_ref[...].astype(BF16)).astype(o_ref.dtype)


def _matmul(a, w, layer, n_cols, tm_candidates, tn_candidates, name):
    m, k = a.shape
    tm = _pick(m, tm_candidates)
    tn = _pick(n_cols, tn_candidates)
    wsz = jnp.dtype(w.dtype).itemsize
    need = tm * k * 2 + 2 * k * tn * wsz + k * tn * 2 + 3 * tm * tn * 4
    return pl.pallas_call(
        _mm_kernel,
        grid=(m // tm, n_cols // tn),
        in_specs=[pl.BlockSpec((tm, k), lambda i, j: (i, 0), pipeline_mode=pl.Buffered(1)),
                  pl.BlockSpec((None, k, tn), lambda i, j: (layer, 0, j))],
        out_specs=pl.BlockSpec((tm, tn), lambda i, j: (i, j)),
        out_shape=jax.ShapeDtypeStruct((m, n_cols), F32),
        compiler_params=pltpu.CompilerParams(
            dimension_semantics=_ARB2, vmem_limit_bytes=_vmem_limit(need)),
        name=name,
    )(a, w)


def _tile_row_index(n, d):
    return lax.broadcasted_iota(jnp.int32, (n, d), 0) & (SUBLANES - 1)


def _tile_roll(x, s):
    n, d = x.shape
    return pltpu.roll(x.reshape(n // SUBLANES, SUBLANES, d), s, 1).reshape(n, d)


def _tile_cumsum(g, rowi):
    c = g
    s = 1
    while s < SUBLANES:
        c = c + jnp.where(rowi >= s, _tile_roll(c, s), 0.0)
        s *= 2
    return c


def _gla_chunk(q, k, v, g, st):
    L, dk = q.shape
    nt = L // SUBLANES
    rowk = _tile_row_index(L, dk)

    c = _tile_cumsum(g, rowk)
    tot = [c[SUBLANES * i + SUBLANES - 1:SUBLANES * (i + 1), :] for i in range(nt)]
    totb = jnp.concatenate([jnp.broadcast_to(t, (SUBLANES, dk)) for t in tot], axis=0)
    qt = q * jnp.exp(c)
    knew = k * jnp.exp(totb - c)

    row = lax.broadcasted_iota(jnp.int32, (L, L), 0)
    col = lax.broadcasted_iota(jnp.int32, (L, L), 1)
    offs = jnp.where((row >> 3) == (col >> 3), row - col, -1)
    eg = jnp.exp(g)
    w = k
    band = jnp.where(offs == 0, jnp.sum(q * w, axis=1, keepdims=True), 0.0)
    for d in range(1, SUBLANES):
        w = _tile_roll(w, 1) * eg
        band = jnp.where(offs == d, jnp.sum(q * w, axis=1, keepdims=True), band)

    tiles = []
    arows = [jnp.zeros((SUBLANES, L), F32)]
    qhat = [qt[0:SUBLANES]]
    er = None
    for i in range(nt):
        lo, hi = SUBLANES * i, SUBLANES * (i + 1)
        if i > 0:
            pad = jnp.zeros((L - lo, dk), F32)
            kh = jnp.concatenate(tiles + [pad], axis=0).astype(BF16)
            arows.append(_dot_nt(qt[lo:hi].astype(BF16), kh))
            qhat.append(qt[lo:hi] * er)
        dcy = jnp.exp(tot[i])
        tiles = [t * dcy for t in tiles] + [knew[lo:hi]]
        er = dcy if er is None else er * dcy

    khat = jnp.concatenate(tiles, axis=0).astype(BF16)
    qh = jnp.concatenate(qhat, axis=0).astype(BF16)
    scores = (jnp.concatenate(arows, axis=0) + band).astype(BF16)
    vb = v.astype(BF16)
    o = _dot(scores, vb) + _dot_nt(qh, st.astype(BF16))
    st_new = st * er + _dot_tn(vb, khat)
    return o, st_new


def _head_norm_gate(o, w, gate):
    on = o * lax.rsqrt(jnp.mean(o * o, axis=-1, keepdims=True) + EPS)
    return on * w * _silu(gate)


def _hgrn_kernel(q_ref, f_ref, i_ref, gate_ref, lb_ref, nw_ref, s0_ref,
                 y_ref, sout_ref, st_ref, *, chunk):
    t = pl.program_id(2)
    tb = q_ref.shape[0]
    nh, dk, dv = s0_ref.shape

    @pl.when(t == 0)
    def _():
        for h in range(nh):
            st_ref[h] = s0_ref[h].T

    def body(ci, carry):
        rows = pl.ds(pl.multiple_of(ci * chunk, chunk), chunk)
        for h in range(nh):
            cols = slice(h * dk, (h + 1) * dk)
            lb = lb_ref[:, cols]
            f = lb + (1.0 - lb) * _sigmoid(f_ref[rows, cols])
            g = jnp.log(jnp.maximum(f, F_FLOOR))
            o, st_new = _gla_chunk(_silu(q_ref[rows, cols]), 1.0 - f, i_ref[rows, cols], g, st_ref[h])
            st_ref[h] = st_new
            y = _head_norm_gate(o, nw_ref[:, cols], gate_ref[rows, cols])
            y_ref[rows, cols] = y.astype(y_ref.dtype)
        return carry

    lax.fori_loop(0, tb // chunk, body, 0, unroll=2)

    @pl.when(t == pl.num_programs(2) - 1)
    def _():
        for h in range(nh):
            sout_ref[h] = st_ref[h].T


def _hgrn(proj, row0, nb, t_len, lb, hg_norm, s0, col0):
    _, nh, dk, dv = s0.shape
    w = nh * dk
    hpb = _pick(nh, (4, 2, 1))
    bw = hpb * dk
    tb = _pick(t_len, (512, 256, 128, 64, 32))
    chunk = min(LA_CHUNK, tb)
    nt = t_len // tb
    rb0 = row0 // tb

    def col(g):
        return pl.BlockSpec((tb, bw), lambda b, h, t: (rb0 + b * nt + t, (col0 + g * w) // bw + h))

    vec = pl.BlockSpec((1, bw), lambda b, h, t: (0, h))
    st = pl.BlockSpec((None, hpb, dk, dv), lambda b, h, t: (b, h, 0, 0))
    y, s_new = pl.pallas_call(
        functools.partial(_hgrn_kernel, chunk=chunk),
        grid=(nb, nh // hpb, nt),
        in_specs=[col(0), col(1), col(2), col(3), vec, vec, st],
        out_specs=[pl.BlockSpec((tb, bw), lambda b, h, t: (b * nt + t, h)), st],
        out_shape=[jax.ShapeDtypeStruct((nb * t_len, w), BF16),
                   jax.ShapeDtypeStruct(s0.shape, F32)],
        scratch_shapes=[pltpu.VMEM((hpb, dv, dk), F32)],
        compiler_params=pltpu.CompilerParams(dimension_semantics=_ARB3),
        name="hgrn2",
    )(proj, proj, proj, proj, lb.reshape(1, w), hg_norm.reshape(1, w), s0)
    return y, s_new


def _gla_kernel(q_ref, k_ref, v_ref, ga_ref, gb_ref, lr_ref, w2_ref, b2_ref, nw_ref, s0_ref,
                y_ref, sout_ref, st_ref, *, chunk, scale, shift):
    t = pl.program_id(2)
    tb = q_ref.shape[0]

    @pl.when(t == 0)
    def _():
        st_ref[...] = s0_ref[...].T

    w2 = w2_ref[...].astype(BF16)
    b2 = b2_ref[...]
    nw = nw_ref[...]

    def body(ci, carry):
        rows = pl.ds(pl.multiple_of(ci * chunk, chunk), chunk)
        logits = _dot(lr_ref[rows, :].astype(BF16), w2) + b2
        g = (jnp.minimum(logits, 0.0) - jnp.log1p(jnp.exp(-jnp.abs(logits)))) / GLA_TAU
        o, st_new = _gla_chunk(q_ref[rows, :] * scale, k_ref[rows, :], v_ref[rows, :], g, st_ref[...])
        st_ref[...] = st_new
        gate = _shifted(ga_ref[rows, :], gb_ref[rows, :], shift)
        y_ref[rows, :] = _head_norm_gate(o, nw, gate).astype(y_ref.dtype)
        return carry

    lax.fori_loop(0, tb // chunk, body, 0)

    @pl.when(t == pl.num_programs(2) - 1)
    def _():
        sout_ref[...] = st_ref[...].T


def _gla(proj, small, row0, nb, t_len, w2p, b2, gla_norm, s0, col_q, col_k, col_v, col_g, shift):
    _, nh, dk, dv = s0.shape
    tb = _pick(t_len, (256, 128, 64, 32))
    chunk = min(LA_CHUNK, tb)
    nt = t_len // tb
    rb0 = row0 // tb
    sw = small.shape[1]

    def col(c0, wd, nxt=0):
        return pl.BlockSpec((tb, wd), lambda b, h, t: (rb0 + b * nt + t, c0 // wd + h + nxt))

    st = pl.BlockSpec((None, None, dk, dv), lambda b, h, t: (b, h, 0, 0))
    gate_next = pl.BlockSpec((tb, LANES),
                             lambda b, h, t: (rb0 + b * nt + t, (col_g + (h + 1) * dv) // LANES))
    y, s_new = pl.pallas_call(
        functools.partial(_gla_kernel, chunk=chunk, scale=float(dk) ** -0.5, shift=shift),
        grid=(nb, nh, nt),
        in_specs=[col(col_q, dk), col(col_k, dk), col(col_v, dv), col(col_g, dv), gate_next,
                  pl.BlockSpec((tb, sw), lambda b, h, t: (rb0 + b * nt + t, 0)),
                  pl.BlockSpec((sw, dk), lambda b, h, t: (0, h)),
                  pl.BlockSpec((1, dk), lambda b, h, t: (0, h)),
                  pl.BlockSpec((1, dv), lambda b, h, t: (0, h)),
                  st],
        out_specs=[pl.BlockSpec((tb, dv), lambda b, h, t: (b * nt + t, h)), st],
        out_shape=[jax.ShapeDtypeStruct((nb * t_len, nh * dv), BF16),
                   jax.ShapeDtypeStruct(s0.shape, F32)],
        scratch_shapes=[pltpu.VMEM((dv, dk), F32)],
        compiler_params=pltpu.CompilerParams(dimension_semantics=_ARB3),
        name="gla",
    )(proj, proj, proj, proj, proj, small, w2p, b2.reshape(1, nh * dk),
      gla_norm.reshape(1, nh * dv), s0)
    return y, s_new


def _lru_kernel(x_ref, gate_ref, cs_ref, h0_ref, cw_ref, cb_ref, wa_ref, ba_ref, wx_ref, bx_ref,
                lam_ref, y_ref, hout_ref, cout_ref, hist_ref, hcar_ref, a_ref, u_ref, h_ref):
    t = pl.program_id(2)
    tb, wd = x_ref.shape
    ncw = cw_ref.shape[0]
    nblk, bd, _ = wa_ref.shape

    @pl.when(t == 0)
    def _():
        hist_ref[...] = jnp.zeros_like(hist_ref)
        hist_ref[SUBLANES - (ncw - 1):SUBLANES, :] = cs_ref[...]
        hcar_ref[...] = jnp.broadcast_to(h0_ref[...], (SUBLANES, wd))

    x = x_ref[...]
    xe = jnp.concatenate([hist_ref[...], x], axis=0)
    cw = cw_ref[...]
    xc = cb_ref[...] + x * cw[ncw - 1:ncw, :]
    for s in range(1, ncw):
        xc = xc + pltpu.roll(xe, s, 0)[SUBLANES:, :] * cw[ncw - 1 - s:ncw - s, :]
    hist_ref[...] = x[tb - SUBLANES:tb, :]

    xb = xc.astype(BF16)
    ra, ri = [], []
    for j in range(nblk):
        xj = xb[:, j * bd:(j + 1) * bd]
        ra.append(_dot(xj, wa_ref[j].astype(BF16)))
        ri.append(_dot(xj, wx_ref[j].astype(BF16)))
    r = _sigmoid(jnp.concatenate(ra, axis=1) + ba_ref[...])
    gi = _sigmoid(jnp.concatenate(ri, axis=1) + bx_ref[...])
    lam = lam_ref[...]
    softplus_neg = jnp.maximum(-lam, 0.0) + jnp.log1p(jnp.exp(-jnp.abs(lam)))
    log_a = -LRU_C * r * softplus_neg
    a = jnp.exp(log_a)
    x2 = 2.0 * log_a
    e2 = jnp.exp(x2)
    small = jnp.where(e2 == 1.0, -x2, (1.0 - e2) * x2 / jnp.log(jnp.where(e2 > 0.25, e2, 0.5)))
    one_m_a2 = jnp.where(x2 < -0.5, 1.0 - e2, small)
    u = jnp.sqrt(jnp.maximum(one_m_a2, 0.0)) * (gi * xc)

    rowi = _tile_row_index(tb, wd)
    s = 1
    while s < SUBLANES:
        ok = rowi >= s
        u = u + jnp.where(ok, a * _tile_roll(u, s), 0.0)
        a = a * jnp.where(ok, _tile_roll(a, s), 1.0)
        s *= 2
    a_ref[...] = a
    u_ref[...] = u

    def body(i, hp):
        rows = pl.ds(pl.multiple_of(i * SUBLANES, SUBLANES), SUBLANES)
        h = u_ref[rows, :] + a_ref[rows, :] * hp
        h_ref[rows, :] = h
        return jnp.broadcast_to(h[SUBLANES - 1:SUBLANES, :], (SUBLANES, wd))

    hp = lax.fori_loop(0, tb // SUBLANES, body, hcar_ref[...], unroll=4)
    hcar_ref[...] = hp
    y_ref[...] = (h_ref[...] * _silu(gate_ref[...])).astype(y_ref.dtype)

    @pl.when(t == pl.num_programs(2) - 1)
    def _():
        hout_ref[...] = hp[0:1, :]
        cout_ref[...] = x[tb - (ncw - 1):tb, :]


def _lru(proj, row0, nb, t_len, conv_state, h0, conv_w, conv_b, wa, ba, wx, bx, lam, col_x, col_g):
    nblk, bd, _ = wa.shape
    w = nblk * bd
    ncw = conv_w.shape[0]
    bps = _pick(nblk, (4, 2, 1))
    bw = bps * bd
    tb = _pick(t_len, (256, 128, 64, 32))
    nt = t_len // tb
    rb0 = row0 // tb

    def col(c0):
        return pl.BlockSpec((tb, bw), lambda b, h, t: (rb0 + b * nt + t, c0 // bw + h))

    vec = pl.BlockSpec((1, bw), lambda b, h, t: (0, h))
    blk = pl.BlockSpec((bps, bd, bd), lambda b, h, t: (h, 0, 0))
    hspec = pl.BlockSpec((None, 1, bw), lambda b, h, t: (b, 0, h))
    cspec = pl.BlockSpec((None, ncw - 1, bw), lambda b, h, t: (b, 0, h))
    y, h_new, c_new = pl.pallas_call(
        _lru_kernel,
        grid=(nb, nblk // bps, nt),
        in_specs=[col(col_x), col(col_g), cspec, hspec,
                  pl.BlockSpec((ncw, bw), lambda b, h, t: (0, h)), vec,
                  blk, vec, blk, vec, vec],
        out_specs=[pl.BlockSpec((tb, bw), lambda b, h, t: (b * nt + t, h)), hspec, cspec],
        out_shape=[jax.ShapeDtypeStruct((nb * t_len, w), BF16),
                   jax.ShapeDtypeStruct((nb, 1, w), F32),
                   jax.ShapeDtypeStruct((nb, ncw - 1, w), F32)],
        scratch_shapes=[pltpu.VMEM((SUBLANES, bw), F32), pltpu.VMEM((SUBLANES, bw), F32),
                        pltpu.VMEM((tb, bw), F32), pltpu.VMEM((tb, bw), F32),
                        pltpu.VMEM((tb, bw), F32)],
        compiler_params=pltpu.CompilerParams(dimension_semantics=_ARB3),
        name="rglru",
    )(proj, proj, conv_state, h0.reshape(nb, 1, w), conv_w, conv_b.reshape(1, w),
      wa, ba.reshape(1, w), wx, bx.reshape(1, w), lam.reshape(1, w))
    return y, h_new.reshape(nb, w), c_new


def _merge_kernel(ya_ref, yb_ref, yc_ref, w_ref, m0_ref, n0_ref, m1_ref, n1_ref, m2_ref, n2_ref,
                  tail_ref, o_ref, *, shift):
    last = pl.program_id(1) == pl.num_programs(1) - 1
    n2 = jnp.where(last, tail_ref[...], n2_ref[...])
    acc = _sigmoid(_shifted(m0_ref[...], n0_ref[...], shift)) * _dot(ya_ref[...], w_ref[0].astype(BF16))
    acc = acc + _sigmoid(_shifted(m1_ref[...], n1_ref[...], shift)) * _dot(yb_ref[...], w_ref[1].astype(BF16))
    acc = acc + _sigmoid(_shifted(m2_ref[...], n2, shift)) * _dot(yc_ref[...], w_ref[2].astype(BF16))
    o_ref[...] = acc.astype(o_ref.dtype)


def _merge(ya, yb, yc, wbr, layer, proj, small, col_m, shift):
    m, bw = ya.shape
    _, nbr, _, d = wbr.shape
    tm = _pick(m, (1056, 1024, 512, 320, 256, 128, 64, 32, 16))
    tn = next(c for c in (256, 128) if d % c == 0 and col_m % c == 0)
    nj = d // tn
    nlb = proj.shape[1] // LANES
    yspec = pl.BlockSpec((tm, bw), lambda i, j: (i, 0), pipeline_mode=pl.Buffered(1))

    def mg(n):
        return pl.BlockSpec((tm, tn), lambda i, j: (i, (col_m + n * d) // tn + j))

    def nx(n):
        return pl.BlockSpec(
            (tm, LANES),
            lambda i, j: (i, jnp.minimum((col_m + n * d + tn) // LANES + j * (tn // LANES), nlb - 1)))

    need = (3 * tm * bw * 2 + 2 * nbr * bw * tn * 4 + nbr * bw * tn * 2
            + 2 * 3 * tm * (tn + LANES) * 4 + 2 * tm * LANES * 4 + 6 * tm * tn * 4)
    return pl.pallas_call(
        functools.partial(_merge_kernel, shift=shift),
        grid=(m // tm, nj),
        in_specs=[yspec, yspec, yspec,
                  pl.BlockSpec((None, nbr, bw, tn), lambda i, j: (layer, 0, 0, j)),
                  mg(0), nx(0), mg(1), nx(1), mg(2), nx(2),
                  pl.BlockSpec((tm, small.shape[1]), lambda i, j: (i, 0))],
        out_specs=pl.BlockSpec((tm, tn), lambda i, j: (i, j)),
        out_shape=jax.ShapeDtypeStruct((m, d), BF16),
        compiler_params=pltpu.CompilerParams(
            dimension_semantics=_ARB2, vmem_limit_bytes=_vmem_limit(need)),
        name="merge",
    )(ya, yb, yc, wbr, proj, proj, proj, proj, proj, proj, small)


def kernel(x_prompt, x_sample, state_hgrn, state_lru_h, state_lru_conv, state_gla, norm_pre, norm_post, w_in, hg_lb_logits, hg_norm, lru_conv_w, lru_conv_b, lru_wa, lru_ba, lru_wx, lru_bx, lru_lambda, gla_w2, gla_b2, gla_norm, w_branch, w_out):
    bp, tp, d = x_prompt.shape
    bs, ts, _ = x_sample.shape
    depth, _, in_cols = w_in.shape
    _, _, hg_h, hg_dk, hg_dv = state_hgrn.shape
    hg_w = hg_h * hg_dk
    lru_w = state_lru_h.shape[-1]
    _, _, gl_h, gl_dk, gl_dv = state_gla.shape
    gl_kw, gl_vw = gl_h * gl_dk, gl_h * gl_dv
    rank = gla_w2.shape[1]
    dt = x_prompt.dtype

    c_hg = 0
    c_lx = 4 * hg_w
    c_lg = c_lx + lru_w
    c_q = c_lg + lru_w
    c_k = c_q + gl_kw
    c_v = c_k + gl_kw
    c_lr = c_v + gl_vw
    c_cg = c_lr
    c_m = c_cg + gl_vw
    n_main = in_cols - rank
    assert n_main % LANES == 0 and c_lr % LANES == 0 and 2 * rank <= LANES
    assert c_m + 3 * d == n_main

    lb_sm = jax.nn.softmax(hg_lb_logits.astype(F32), axis=0)
    lb_all = jnp.cumsum(lb_sm, axis=0) - lb_sm[0:1]

    w_small = jnp.concatenate([w_in[:, :, n_main:], w_in[:, :, c_lr:c_lr + rank]], axis=2)
    w_small = jnp.pad(w_small, ((0, 0), (0, 0), (0, LANES - 2 * rank))).astype(BF16)
    w2p = jnp.pad(gla_w2, ((0, 0), (rank, LANES - 2 * rank), (0, 0)))

    mp, ms = bp * tp, bs * ts
    x = jnp.concatenate([x_prompt.reshape(mp, d), x_sample.reshape(ms, d)], axis=0)
    big_tm = (1408, 1056, 1024, 512, 320, 256, 128, 64, 32, 16)

    zeros = lambda shape: jnp.zeros(shape, dt)
    outs = {k: [] for k in ("hg_p", "hg_s", "lh_p", "lh_s", "lc_p", "lc_s", "gl_p", "gl_s")}
    for l in range(depth):
        z = _prenorm(x, norm_pre[l])
        proj = _matmul(z, w_in, l, n_main, big_tm, (512, 256, 128), "in_proj")
        small = _matmul(z, w_small, l, LANES, big_tm, (LANES,), "small_proj")

        groups = (
            ("p", 0, bp, tp, zeros((bp,) + state_hgrn.shape[2:]), zeros((bp, lru_w)),
             zeros((bp,) + state_lru_conv.shape[2:]), zeros((bp,) + state_gla.shape[2:])),
            ("s", mp, bs, ts, state_hgrn[l], state_lru_h[l], state_lru_conv[l], state_gla[l]),
        )
        ya, yb, yc = [], [], []
        for tag, row0, nb, t_len, s_hg, s_lh, s_lc, s_gl in groups:
            y, s = _hgrn(proj, row0, nb, t_len, lb_all[l], hg_norm[l], s_hg, c_hg)
            ya.append(y); outs["hg_" + tag].append(s)
            y, hn, cn = _lru(proj, row0, nb, t_len, s_lc, s_lh, lru_conv_w[l], lru_conv_b[l],
                             lru_wa[l], lru_ba[l], lru_wx[l], lru_bx[l], lru_lambda[l], c_lx, c_lg)
            yb.append(y); outs["lh_" + tag].append(hn); outs["lc_" + tag].append(cn)
            y, s = _gla(proj, small, row0, nb, t_len, w2p[l], gla_b2[l], gla_norm[l], s_gl,
                        c_q, c_k, c_v, c_cg, rank)
            yc.append(y); outs["gl_" + tag].append(s)

        merged = _merge(jnp.concatenate(ya, axis=0), jnp.concatenate(yb, axis=0),
                        jnp.concatenate(yc, axis=0), w_branch, l, proj, small, c_m, rank)
        out = _matmul(merged, w_out, l, d, big_tm, (512, 256, 128), "out_proj")
        x = _postnorm(x, out, norm_post[l])

    yp = x[:mp].reshape(bp, tp, d)
    ys = x[mp:].reshape(bs, ts, d)
    st = lambda k: jnp.stack(outs[k])
    return (yp, ys, st("hg_p"), st("hg_s"), st("lh_p"), st("lh_s"),
            st("lc_p"), st("lc_s"), st("gl_p"), st("gl_s"))
```

```python
import functools

import jax
import jax.numpy as jnp
from jax import lax
from jax.experimental import pallas as pl
from jax.experimental.pallas import tpu as pltpu

F32 = jnp.float32
BF16 = jnp.bfloat16

EPS = 1e-6
F_FLOOR = 1e-6
LRU_C = 8.0
GLA_TAU = 16.0
SUBLANES = 8
LANES = 128
LA_CHUNK = 64
V7X_VMEM_LIMIT = 60000 * 1024


def _vmem_limit(nbytes):
    return int(min(V7X_VMEM_LIMIT, max(32 * 1024 * 1024, nbytes * 5 // 4 + (4 << 20))))


def _pick(n, candidates):
    for c in candidates:
        if n % c == 0:
            return c
    return n


def _dot(a, b):
    return jnp.dot(a, b, preferred_element_type=F32)


def _dot_nt(a, b):
    return lax.dot_general(a, b, (((1,), (1,)), ((), ())), preferred_element_type=F32)


def _dot_tn(a, b):
    return lax.dot_general(a, b, (((0,), (0,)), ((), ())), preferred_element_type=F32)


def _sigmoid(x):
    return jax.nn.sigmoid(x)


def _silu(x):
    return x * jax.nn.sigmoid(x)


def _without_ref(kernel, idx):
    def wrapped(*refs):
        return kernel(*refs[:idx], *refs[idx + 1:])
    return wrapped


def _alias_spec(y_prev):
    return [] if y_prev is None else [pl.BlockSpec(memory_space=pl.ANY)]


def _alias_arg(y_prev):
    return [] if y_prev is None else [y_prev]


def _alias_map(y_prev, idx):
    return {} if y_prev is None else {idx: 0}


_ARB2 = ("arbitrary", "arbitrary")
_ARB3 = ("arbitrary", "arbitrary", "arbitrary")


def _rms(x, w):
    return x * lax.rsqrt(jnp.mean(x * x, axis=-1, keepdims=True) + EPS) * w


def _group_specs(mp, ms, d):
    tr = next(c for c in (256, 128, 64, 32, 16, 8) if mp % c == 0 and ms % c == 0)
    nbp = mp // tr
    pspec = pl.BlockSpec((tr, d), lambda i: (jnp.minimum(i, nbp - 1), 0))
    sspec = pl.BlockSpec((tr, d), lambda i: (jnp.maximum(i - nbp, 0), 0))
    return tr, nbp, pspec, sspec


def _prenorm2_kernel(xp_ref, xs_ref, w_ref, z_ref, *, nbp):
    x = jnp.where(pl.program_id(0) < nbp, xp_ref[...], xs_ref[...])
    z_ref[...] = _rms(x, w_ref[...]).astype(z_ref.dtype)


def _prenorm_kernel(x_ref, w_ref, z_ref):
    z_ref[...] = _rms(x_ref[...], w_ref[...]).astype(z_ref.dtype)


def _prenorm(xs, w):
    d = w.shape[0]
    vec = pl.BlockSpec((1, d), lambda i: (0, 0))
    params = pltpu.CompilerParams(dimension_semantics=("arbitrary",))
    if len(xs) == 2:
        mp, ms = xs[0].shape[0], xs[1].shape[0]
        tr, nbp, pspec, sspec = _group_specs(mp, ms, d)
        return pl.pallas_call(
            functools.partial(_prenorm2_kernel, nbp=nbp),
            grid=((mp + ms) // tr,),
            in_specs=[pspec, sspec, vec],
            out_specs=pl.BlockSpec((tr, d), lambda i: (i, 0)),
            out_shape=jax.ShapeDtypeStruct((mp + ms, d), BF16),
            compiler_params=params, name="prenorm",
        )(xs[0], xs[1], w.reshape(1, d))
    m = xs[0].shape[0]
    tr = _pick(m, (256, 128, 64, 32, 16))
    return pl.pallas_call(
        _prenorm_kernel,
        grid=(m // tr,),
        in_specs=[pl.BlockSpec((tr, d), lambda i: (i, 0)), vec],
        out_specs=pl.BlockSpec((tr, d), lambda i: (i, 0)),
        out_shape=jax.ShapeDtypeStruct((m, d), BF16),
        compiler_params=params, name="prenorm",
    )(xs[0], w.reshape(1, d))


def _postnorm_kernel(*refs, nbp, n_in, n_out):
    i = pl.program_id(0)
    o_ref, w_ref = refs[n_in], refs[n_in + 1]
    outs = refs[n_in + 2:]
    x = refs[0][...] if n_in == 1 else jnp.where(i < nbp, refs[0][...], refs[1][...])
    y = x + _rms(o_ref[...], w_ref[...])
    if n_out == 1:
        outs[0][...] = y
    else:
        @pl.when(i < nbp)
        def _():
            outs[0][...] = y

        @pl.when(i >= nbp)
        def _():
            outs[1][...] = y


def _postnorm(xs, out, w, mp, ms, split_out):
    d = w.shape[0]
    tr, nbp, pspec, sspec = _group_specs(mp, ms, d)
    full = pl.BlockSpec((tr, d), lambda i: (i, 0))
    in_specs = ([pspec, sspec] if len(xs) == 2 else [full]) + [full, pl.BlockSpec((1, d), lambda i: (0, 0))]
    if split_out:
        out_specs = [pspec, sspec]
        out_shape = [jax.ShapeDtypeStruct((mp, d), F32), jax.ShapeDtypeStruct((ms, d), F32)]
    else:
        out_specs = [full]
        out_shape = [jax.ShapeDtypeStruct((mp + ms, d), F32)]
    return pl.pallas_call(
        functools.partial(_postnorm_kernel, nbp=nbp, n_in=len(xs), n_out=len(out_specs)),
        grid=((mp + ms) // tr,),
        in_specs=in_specs, out_specs=out_specs, out_shape=out_shape,
        compiler_params=pltpu.CompilerParams(dimension_semantics=("arbitrary",)),
        name="postnorm",
    )(*xs, out, w.reshape(1, d))


def _mm_kernel(a_ref, w_ref, o_ref):
    o_ref[...] = _dot(a_ref[...], w_ref[...].astype(BF16))


def _mm_t_kernel(a_ref, wt_ref, o_ref):
    o_ref[...] = _dot_nt(a_ref[...], wt_ref[0].astype(BF16))


BIG_TM = (1408, 1056, 1024, 512, 320, 256, 128, 64, 32, 16)


def _matmul(a, w, layer, name):
    m, k = a.shape
    n = w.shape[2]
    tm = _pick(m, BIG_TM)
    tn = _pick(n, (512, 256, 128))
    need = tm * k * 2 + 2 * k * tn * 4 + k * tn * 2 + 3 * tm * tn * 4
    return pl.pallas_call(
        _mm_kernel,
        grid=(m // tm, n // tn),
        in_specs=[pl.BlockSpec((tm, k), lambda i, j: (i, 0), pipeline_mode=pl.Buffered(1)),
                  pl.BlockSpec((None, k, tn), lambda i, j: (layer, 0, j))],
        out_specs=pl.BlockSpec((tm, tn), lambda i, j: (i, j)),
        out_shape=jax.ShapeDtypeStruct((m, n), F32),
        compiler_params=pltpu.CompilerParams(
            dimension_semantics=_ARB2, vmem_limit_bytes=_vmem_limit(need)),
        name=name,
    )(a, w)


def _matmul_t(a, wt, layer, n_out, skip_start, skip, name):
    m, k = a.shape
    tm = _pick(m, BIG_TM)
    tn = next(c for c in (512, 256, 128, 64, 32, 16, 8)
              if n_out % c == 0 and skip_start % c == 0 and (c <= n_out // 2 or c == n_out))
    assert skip % SUBLANES == 0

    def w_index(i, j):
        row = j * tn + jnp.where(j * tn >= skip_start, skip, 0)
        return layer, pl.multiple_of(row, SUBLANES), 0

    need = tm * k * 2 + 2 * k * tn * 4 + k * tn * 2 + 3 * tm * tn * 4
    return pl.pallas_call(
        _mm_t_kernel,
        grid=(m // tm, n_out // tn),
        in_specs=[pl.BlockSpec((tm, k), lambda i, j: (i, 0), pipeline_mode=pl.Buffered(1)),
                  pl.BlockSpec((pl.Element(1), pl.Element(tn), pl.Element(k)), w_index)],
        out_specs=pl.BlockSpec((tm, tn), lambda i, j: (i, j)),
        out_shape=jax.ShapeDtypeStruct((m, n_out), F32),
        compiler_params=pltpu.CompilerParams(
            dimension_semantics=_ARB2, vmem_limit_bytes=_vmem_limit(need)),
        name=name,
    )(a, wt)


def _tile_row_index(n, d):
    return lax.broadcasted_iota(jnp.int32, (n, d), 0) & (SUBLANES - 1)


def _tile_roll(x, s):
    n, d = x.shape
    return pltpu.roll(x.reshape(n // SUBLANES, SUBLANES, d), s, 1).reshape(n, d)


def _tile_cumsum(g, rowi):
    c = g
    s = 1
    while s < SUBLANES:
        c = c + jnp.where(rowi >= s, _tile_roll(c, s), 0.0)
        s *= 2
    return c


def _gla_chunk(q, k, v, g, st):
    L, dk = q.shape
    nt = L // SUBLANES
    rowk = _tile_row_index(L, dk)

    c = _tile_cumsum(g, rowk)
    tot = [c[SUBLANES * i + SUBLANES - 1:SUBLANES * (i + 1), :] for i in range(nt)]
    totb = jnp.concatenate([jnp.broadcast_to(t, (SUBLANES, dk)) for t in tot], axis=0)
    qt = q * jnp.exp(c)
    knew = k * jnp.exp(totb - c)

    row = lax.broadcasted_iota(jnp.int32, (L, L), 0)
    col = lax.broadcasted_iota(jnp.int32, (L, L), 1)
    offs = jnp.where((row >> 3) == (col >> 3), row - col, -1)
    eg = jnp.exp(g)
    w = k
    band = jnp.where(offs == 0, jnp.sum(q * w, axis=1, keepdims=True), 0.0)
    for d in range(1, SUBLANES):
        w = _tile_roll(w, 1) * eg
        band = jnp.where(offs == d, jnp.sum(q * w, axis=1, keepdims=True), band)

    tiles = []
    arows = [jnp.zeros((SUBLANES, L), F32)]
    qhat = [qt[0:SUBLANES]]
    er = None
    for i in range(nt):
        lo, hi = SUBLANES * i, SUBLANES * (i + 1)
        if i > 0:
            pad = jnp.zeros((L - lo, dk), F32)
            kh = jnp.concatenate(tiles + [pad], axis=0).astype(BF16)
            arows.append(_dot_nt(qt[lo:hi].astype(BF16), kh))
            qhat.append(qt[lo:hi] * er)
        dcy = jnp.exp(tot[i])
        tiles = [t * dcy for t in tiles] + [knew[lo:hi]]
        er = dcy if er is None else er * dcy

    khat = jnp.concatenate(tiles, axis=0).astype(BF16)
    qh = jnp.concatenate(qhat, axis=0).astype(BF16)
    scores = (jnp.concatenate(arows, axis=0) + band).astype(BF16)
    vb = v.astype(BF16)
    o = _dot(scores, vb) + _dot_nt(qh, st.astype(BF16))
    st_new = st * er + _dot_tn(vb, khat)
    return o, st_new


def _head_norm_gate(o, w, gate):
    on = o * lax.rsqrt(jnp.mean(o * o, axis=-1, keepdims=True) + EPS)
    return on * w * _silu(gate)


def _hgrn_kernel(q_ref, f_ref, i_ref, gate_ref, lb_ref, nw_ref, s0_ref,
                 y_ref, sout_ref, st_ref, *, chunk):
    t = pl.program_id(2)
    tb = q_ref.shape[0]
    nh, dk, dv = s0_ref.shape

    @pl.when(t == 0)
    def _():
        for h in range(nh):
            st_ref[h] = s0_ref[h].T

    def body(ci, carry):
        rows = pl.ds(pl.multiple_of(ci * chunk, chunk), chunk)
        for h in range(nh):
            cols = slice(h * dk, (h + 1) * dk)
            lb = lb_ref[:, cols]
            f = lb + (1.0 - lb) * _sigmoid(f_ref[rows, cols])
            g = jnp.log(jnp.maximum(f, F_FLOOR))
            o, st_new = _gla_chunk(_silu(q_ref[rows, cols]), 1.0 - f, i_ref[rows, cols], g, st_ref[h])
            st_ref[h] = st_new
            y = _head_norm_gate(o, nw_ref[:, cols], gate_ref[rows, cols])
            y_ref[rows, cols] = y.astype(y_ref.dtype)
        return carry

    lax.fori_loop(0, tb // chunk, body, 0, unroll=2)

    @pl.when(t == pl.num_programs(2) - 1)
    def _():
        for h in range(nh):
            sout_ref[h] = st_ref[h].T


def _hgrn(proj, row0, nb, t_len, lb, hg_norm, s0, col0, y_prev):
    _, nh, dk, dv = s0.shape
    w = nh * dk
    hpb = _pick(nh, (4, 2, 1))
    bw = hpb * dk
    tb = _pick(t_len, (512, 256, 128, 64, 32))
    chunk = min(LA_CHUNK, tb)
    nt = t_len // tb
    rb0 = row0 // tb

    def col(g):
        return pl.BlockSpec((tb, bw), lambda b, h, t: (rb0 + b * nt + t, (col0 + g * w) // bw + h))

    vec = pl.BlockSpec((1, bw), lambda b, h, t: (0, h))
    st = pl.BlockSpec((None, hpb, dk, dv), lambda b, h, t: (b, h, 0, 0))
    kern = functools.partial(_hgrn_kernel, chunk=chunk)
    args = [proj, proj, proj, proj, lb.reshape(1, w), hg_norm.reshape(1, w), s0]
    y, s_new = pl.pallas_call(
        kern if y_prev is None else _without_ref(kern, len(args)),
        grid=(nb, nh // hpb, nt),
        in_specs=[col(0), col(1), col(2), col(3), vec, vec, st] + _alias_spec(y_prev),
        out_specs=[pl.BlockSpec((tb, bw), lambda b, h, t: (rb0 + b * nt + t, h)), st],
        out_shape=[jax.ShapeDtypeStruct((proj.shape[0], w), BF16),
                   jax.ShapeDtypeStruct(s0.shape, F32)],
        scratch_shapes=[pltpu.VMEM((hpb, dv, dk), F32)],
        input_output_aliases=_alias_map(y_prev, len(args)),
        compiler_params=pltpu.CompilerParams(dimension_semantics=_ARB3),
        name="hgrn2",
    )(*args, *_alias_arg(y_prev))
    return y, s_new


def _gla_kernel(q_ref, k_ref, v_ref, gate_ref, lr_ref, w2_ref, b2_ref, nw_ref, s0_ref,
                y_ref, sout_ref, st_ref, *, chunk, scale):
    t = pl.program_id(2)
    tb = q_ref.shape[0]

    @pl.when(t == 0)
    def _():
        st_ref[...] = s0_ref[...].T

    w2 = w2_ref[...].astype(BF16)
    b2 = b2_ref[...]
    nw = nw_ref[...]

    def body(ci, carry):
        rows = pl.ds(pl.multiple_of(ci * chunk, chunk), chunk)
        logits = _dot(lr_ref[rows, :].astype(BF16), w2) + b2
        g = (jnp.minimum(logits, 0.0) - jnp.log1p(jnp.exp(-jnp.abs(logits)))) / GLA_TAU
        o, st_new = _gla_chunk(q_ref[rows, :] * scale, k_ref[rows, :], v_ref[rows, :], g, st_ref[...])
        st_ref[...] = st_new
        y_ref[rows, :] = _head_norm_gate(o, nw, gate_ref[rows, :]).astype(y_ref.dtype)
        return carry

    lax.fori_loop(0, tb // chunk, body, 0)

    @pl.when(t == pl.num_programs(2) - 1)
    def _():
        sout_ref[...] = st_ref[...].T


def _gla(proj, lr, row0, nb, t_len, w2, b2, gla_norm, s0, col_q, col_k, col_v, col_g, y_prev):
    _, nh, dk, dv = s0.shape
    rank = lr.shape[1]
    tb = _pick(t_len, (256, 128, 64, 32))
    chunk = min(LA_CHUNK, tb)
    nt = t_len // tb
    rb0 = row0 // tb

    def col(c0, wd):
        return pl.BlockSpec((tb, wd), lambda b, h, t: (rb0 + b * nt + t, c0 // wd + h))

    st = pl.BlockSpec((None, None, dk, dv), lambda b, h, t: (b, h, 0, 0))
    kern = functools.partial(_gla_kernel, chunk=chunk, scale=float(dk) ** -0.5)
    args = [proj, proj, proj, proj, lr, w2, b2.reshape(1, nh * dk),
            gla_norm.reshape(1, nh * dv), s0]
    y, s_new = pl.pallas_call(
        kern if y_prev is None else _without_ref(kern, len(args)),
        grid=(nb, nh, nt),
        in_specs=[col(col_q, dk), col(col_k, dk), col(col_v, dv), col(col_g, dv),
                  pl.BlockSpec((tb, rank), lambda b, h, t: (rb0 + b * nt + t, 0)),
                  pl.BlockSpec((rank, dk), lambda b, h, t: (0, h)),
                  pl.BlockSpec((1, dk), lambda b, h, t: (0, h)),
                  pl.BlockSpec((1, dv), lambda b, h, t: (0, h)),
                  st] + _alias_spec(y_prev),
        out_specs=[pl.BlockSpec((tb, dv), lambda b, h, t: (rb0 + b * nt + t, h)), st],
        out_shape=[jax.ShapeDtypeStruct((proj.shape[0], nh * dv), BF16),
                   jax.ShapeDtypeStruct(s0.shape, F32)],
        scratch_shapes=[pltpu.VMEM((dv, dk), F32)],
        input_output_aliases=_alias_map(y_prev, len(args)),
        compiler_params=pltpu.CompilerParams(dimension_semantics=_ARB3),
        name="gla",
    )(*args, *_alias_arg(y_prev))
    return y, s_new


def _lru_kernel(x_ref, gate_ref, cs_ref, h0_ref, cw_ref, cb_ref, wa_ref, ba_ref, wx_ref, bx_ref,
                lam_ref, y_ref, hout_ref, cout_ref, hist_ref, hcar_ref, a_ref, u_ref, h_ref):
    t = pl.program_id(2)
    tb, wd = x_ref.shape
    ncw = cw_ref.shape[0]
    nblk, bd, _ = wa_ref.shape

    @pl.when(t == 0)
    def _():
        hist_ref[...] = jnp.zeros_like(hist_ref)
        hist_ref[SUBLANES - (ncw - 1):SUBLANES, :] = cs_ref[...]
        hcar_ref[...] = jnp.broadcast_to(h0_ref[...], (SUBLANES, wd))

    x = x_ref[...]
    xe = jnp.concatenate([hist_ref[...], x], axis=0)
    cw = cw_ref[...]
    xc = cb_ref[...] + x * cw[ncw - 1:ncw, :]
    for s in range(1, ncw):
        xc = xc + pltpu.roll(xe, s, 0)[SUBLANES:, :] * cw[ncw - 1 - s:ncw - s, :]
    hist_ref[...] = x[tb - SUBLANES:tb, :]

    xb = xc.astype(BF16)
    ra, ri = [], []
    for j in range(nblk):
        xj = xb[:, j * bd:(j + 1) * bd]
        ra.append(_dot(xj, wa_ref[j].astype(BF16)))
        ri.append(_dot(xj, wx_ref[j].astype(BF16)))
    r = _sigmoid(jnp.concatenate(ra, axis=1) + ba_ref[...])
    gi = _sigmoid(jnp.concatenate(ri, axis=1) + bx_ref[...])
    lam = lam_ref[...]
    softplus_neg = jnp.maximum(-lam, 0.0) + jnp.log1p(jnp.exp(-jnp.abs(lam)))
    log_a = -LRU_C * r * softplus_neg
    a = jnp.exp(log_a)
    x2 = 2.0 * log_a
    e2 = jnp.exp(x2)
    small = jnp.where(e2 == 1.0, -x2, (1.0 - e2) * x2 / jnp.log(jnp.where(e2 > 0.25, e2, 0.5)))
    one_m_a2 = jnp.where(x2 < -0.5, 1.0 - e2, small)
    u = jnp.sqrt(jnp.maximum(one_m_a2, 0.0)) * (gi * xc)

    rowi = _tile_row_index(tb, wd)
    s = 1
    while s < SUBLANES:
        ok = rowi >= s
        u = u + jnp.where(ok, a * _tile_roll(u, s), 0.0)
        a = a * jnp.where(ok, _tile_roll(a, s), 1.0)
        s *= 2
    a_ref[...] = a
    u_ref[...] = u

    def body(i, hp):
        rows = pl.ds(pl.multiple_of(i * SUBLANES, SUBLANES), SUBLANES)
        h = u_ref[rows, :] + a_ref[rows, :] * hp
        h_ref[rows, :] = h
        return jnp.broadcast_to(h[SUBLANES - 1:SUBLANES, :], (SUBLANES, wd))

    hp = lax.fori_loop(0, tb // SUBLANES, body, hcar_ref[...], unroll=4)
    hcar_ref[...] = hp
    y_ref[...] = (h_ref[...] * _silu(gate_ref[...])).astype(y_ref.dtype)

    @pl.when(t == pl.num_programs(2) - 1)
    def _():
        hout_ref[...] = hp[0:1, :]
        cout_ref[...] = x[tb - (ncw - 1):tb, :]


def _lru(proj, row0, nb, t_len, conv_state, h0, conv_w, conv_b, wa, ba, wx, bx, lam, col_x, col_g, y_prev):
    nblk, bd, _ = wa.shape
    w = nblk * bd
    ncw = conv_w.shape[0]
    bps = _pick(nblk, (4, 2, 1))
    bw = bps * bd
    tb = _pick(t_len, (256, 128, 64, 32))
    nt = t_len // tb
    rb0 = row0 // tb

    def col(c0):
        return pl.BlockSpec((tb, bw), lambda b, h, t: (rb0 + b * nt + t, c0 // bw + h))

    vec = pl.BlockSpec((1, bw), lambda b, h, t: (0, h))
    blk = pl.BlockSpec((bps, bd, bd), lambda b, h, t: (h, 0, 0))
    hspec = pl.BlockSpec((None, 1, bw), lambda b, h, t: (b, 0, h))
    cspec = pl.BlockSpec((None, ncw - 1, bw), lambda b, h, t: (b, 0, h))
    args = [proj, proj, conv_state, h0.reshape(nb, 1, w), conv_w, conv_b.reshape(1, w),
            wa, ba.reshape(1, w), wx, bx.reshape(1, w), lam.reshape(1, w)]
    y, h_new, c_new = pl.pallas_call(
        _lru_kernel if y_prev is None else _without_ref(_lru_kernel, len(args)),
        grid=(nb, nblk // bps, nt),
        in_specs=[col(col_x), col(col_g), cspec, hspec,
                  pl.BlockSpec((ncw, bw), lambda b, h, t: (0, h)), vec,
                  blk, vec, blk, vec, vec] + _alias_spec(y_prev),
        out_specs=[pl.BlockSpec((tb, bw), lambda b, h, t: (rb0 + b * nt + t, h)), hspec, cspec],
        out_shape=[jax.ShapeDtypeStruct((proj.shape[0], w), BF16),
                   jax.ShapeDtypeStruct((nb, 1, w), F32),
                   jax.ShapeDtypeStruct((nb, ncw - 1, w), F32)],
        scratch_shapes=[pltpu.VMEM((SUBLANES, bw), F32), pltpu.VMEM((SUBLANES, bw), F32),
                        pltpu.VMEM((tb, bw), F32), pltpu.VMEM((tb, bw), F32),
                        pltpu.VMEM((tb, bw), F32)],
        input_output_aliases=_alias_map(y_prev, len(args)),
        compiler_params=pltpu.CompilerParams(dimension_semantics=_ARB3),
        name="rglru",
    )(*args, *_alias_arg(y_prev))
    return y, h_new.reshape(nb, w), c_new


def _merge_kernel(ya_ref, yb_ref, yc_ref, w_ref, m0_ref, m1_ref, m2_ref, o_ref):
    acc = _sigmoid(m0_ref[...]) * _dot(ya_ref[...], w_ref[0].astype(BF16))
    acc = acc + _sigmoid(m1_ref[...]) * _dot(yb_ref[...], w_ref[1].astype(BF16))
    acc = acc + _sigmoid(m2_ref[...]) * _dot(yc_ref[...], w_ref[2].astype(BF16))
    o_ref[...] = acc.astype(o_ref.dtype)


def _merge(ya, yb, yc, wbr, layer, proj, col_m):
    m, bw = ya.shape
    _, nbr, _, d = wbr.shape
    tm = _pick(m, (1056, 1024, 512, 320, 256, 128, 64, 32, 16))
    tn = next(c for c in (256, 128) if d % c == 0 and col_m % c == 0)
    yspec = pl.BlockSpec((tm, bw), lambda i, j: (i, 0), pipeline_mode=pl.Buffered(1))

    def mg(n):
        return pl.BlockSpec((tm, tn), lambda i, j: (i, (col_m + n * d) // tn + j))

    need = (3 * tm * bw * 2 + 2 * nbr * bw * tn * 4 + nbr * bw * tn * 2
            + 2 * 3 * tm * tn * 4 + 6 * tm * tn * 4)
    return pl.pallas_call(
        _merge_kernel,
        grid=(m // tm, d // tn),
        in_specs=[yspec, yspec, yspec,
                  pl.BlockSpec((None, nbr, bw, tn), lambda i, j: (layer, 0, 0, j)),
                  mg(0), mg(1), mg(2)],
        out_specs=pl.BlockSpec((tm, tn), lambda i, j: (i, j)),
        out_shape=jax.ShapeDtypeStruct((m, d), BF16),
        compiler_params=pltpu.CompilerParams(
            dimension_semantics=_ARB2, vmem_limit_bytes=_vmem_limit(need)),
        name="merge",
    )(ya, yb, yc, wbr, proj, proj, proj)


def kernel(x_prompt, x_sample, state_hgrn, state_lru_h, state_lru_conv, state_gla, norm_pre, norm_post, w_in, hg_lb_logits, hg_norm, lru_conv_w, lru_conv_b, lru_wa, lru_ba, lru_wx, lru_bx, lru_lambda, gla_w2, gla_b2, gla_norm, w_branch, w_out):
    bp, tp, d = x_prompt.shape
    bs, ts, _ = x_sample.shape
    depth, _, in_cols = w_in.shape
    _, _, hg_h, hg_dk, hg_dv = state_hgrn.shape
    hg_w = hg_h * hg_dk
    lru_w = state_lru_h.shape[-1]
    _, _, gl_h, gl_dk, gl_dv = state_gla.shape
    gl_kw, gl_vw = gl_h * gl_dk, gl_h * gl_dv
    rank = gla_w2.shape[1]
    dt = x_prompt.dtype

    c_hg = 0
    c_lx = 4 * hg_w
    c_lg = c_lx + lru_w
    c_q = c_lg + lru_w
    c_k = c_q + gl_kw
    c_v = c_k + gl_kw
    c_lr = c_v + gl_vw
    c_cg = c_lr
    c_m = c_cg + gl_vw
    n_main = in_cols - rank
    assert c_m + 3 * d == n_main

    lb_sm = jax.nn.softmax(hg_lb_logits.astype(F32), axis=0)
    lb_all = jnp.cumsum(lb_sm, axis=0) - lb_sm[0:1]
    w_in_t = jnp.transpose(w_in, (0, 2, 1))

    mp, ms = bp * tp, bs * ts
    xs = (x_prompt.reshape(mp, d), x_sample.reshape(ms, d))

    zeros = lambda shape: jnp.zeros(shape, dt)
    outs = {k: [] for k in ("hg_p", "hg_s", "lh_p", "lh_s", "lc_p", "lc_s", "gl_p", "gl_s")}
    for l in range(depth):
        z = _prenorm(xs, norm_pre[l])
        proj = _matmul_t(z, w_in_t, l, n_main, c_lr, rank, "in_proj")
        lr = _matmul_t(z, w_in_t, l, rank, 0, c_lr, "lr_proj")

        groups = (
            ("p", 0, bp, tp, zeros((bp,) + state_hgrn.shape[2:]), zeros((bp, lru_w)),
             zeros((bp,) + state_lru_conv.shape[2:]), zeros((bp,) + state_gla.shape[2:])),
            ("s", mp, bs, ts, state_hgrn[l], state_lru_h[l], state_lru_conv[l], state_gla[l]),
        )
        ya = yb = yc = None
        for tag, row0, nb, t_len, s_hg, s_lh, s_lc, s_gl in groups:
            ya, s = _hgrn(proj, row0, nb, t_len, lb_all[l], hg_norm[l], s_hg, c_hg, ya)
            outs["hg_" + tag].append(s)
            yb, hn, cn = _lru(proj, row0, nb, t_len, s_lc, s_lh, lru_conv_w[l], lru_conv_b[l],
                              lru_wa[l], lru_ba[l], lru_wx[l], lru_bx[l], lru_lambda[l], c_lx, c_lg, yb)
            outs["lh_" + tag].append(hn); outs["lc_" + tag].append(cn)
            yc, s = _gla(proj, lr, row0, nb, t_len, gla_w2[l], gla_b2[l], gla_norm[l], s_gl,
                         c_q, c_k, c_v, c_cg, yc)
            outs["gl_" + tag].append(s)

        merged = _merge(ya, yb, yc, w_branch, l, proj, c_m)
        out = _matmul(merged, w_out, l, "out_proj")
        xs = _postnorm(xs, out, norm_post[l], mp, ms, split_out=(l == depth - 1))

    yp = xs[0].reshape(bp, tp, d)
    ys = xs[1].reshape(bs, ts, d)
    st = lambda k: jnp.stack(outs[k])
    return (yp, ys, st("hg_p"), st("hg_s"), st("lh_p"), st("lh_s"),
            st("lc_p"), st("lc_s"), st("gl_p"), st("gl_s"))
```

```python
import functools
import math

import jax
import jax.numpy as jnp
from jax import lax
from jax.experimental import pallas as pl
from jax.experimental.pallas import tpu as pltpu

F32 = jnp.float32
BF16 = jnp.bfloat16

EPS = 1e-6
F_FLOOR = 1e-6
LRU_C = 8.0
GLA_TAU = 16.0
SUBLANES = 8
LANES = 128
LA_CHUNK = 64
V7X_VMEM_LIMIT = 60000 * 1024


def _vmem_limit(nbytes):
    return int(min(V7X_VMEM_LIMIT, max(32 * 1024 * 1024, nbytes * 5 // 4 + (4 << 20))))


def _pick(n, candidates):
    for c in candidates:
        if n % c == 0:
            return c
    return n


def _dot(a, b):
    return jnp.dot(a, b, preferred_element_type=F32)


def _dot_nt(a, b):
    return lax.dot_general(a, b, (((1,), (1,)), ((), ())), preferred_element_type=F32)


def _dot_tn(a, b):
    return lax.dot_general(a, b, (((0,), (0,)), ((), ())), preferred_element_type=F32)


def _sigmoid(x):
    return jax.nn.sigmoid(x)


def _silu(x):
    return x * jax.nn.sigmoid(x)


def _without_ref(kernel, idx):
    def wrapped(*refs):
        return kernel(*refs[:idx], *refs[idx + 1:])
    return wrapped


def _alias_spec(y_prev):
    return [] if y_prev is None else [pl.BlockSpec(memory_space=pl.ANY)]


def _alias_arg(y_prev):
    return [] if y_prev is None else [y_prev]


def _alias_map(y_prev, idx):
    return {} if y_prev is None else {idx: 0}


_ARB2 = ("arbitrary", "arbitrary")
_ARB3 = ("arbitrary", "arbitrary", "arbitrary")


def _rms(x, w):
    return x * lax.rsqrt(jnp.mean(x * x, axis=-1, keepdims=True) + EPS) * w


def _group_specs(mp, ms, d):
    tr = next(c for c in (256, 128, 64, 32, 16, 8) if mp % c == 0 and ms % c == 0)
    nbp = mp // tr
    pspec = pl.BlockSpec((tr, d), lambda i: (jnp.minimum(i, nbp - 1), 0))
    sspec = pl.BlockSpec((tr, d), lambda i: (jnp.maximum(i - nbp, 0), 0))
    return tr, nbp, pspec, sspec


def _prenorm2_kernel(xp_ref, xs_ref, w_ref, z_ref, *, nbp):
    x = jnp.where(pl.program_id(0) < nbp, xp_ref[...], xs_ref[...])
    z_ref[...] = _rms(x, w_ref[...]).astype(z_ref.dtype)


def _prenorm(xs, w):
    d = w.shape[0]
    mp, ms = xs[0].shape[0], xs[1].shape[0]
    tr, nbp, pspec, sspec = _group_specs(mp, ms, d)
    return pl.pallas_call(
        functools.partial(_prenorm2_kernel, nbp=nbp),
        grid=((mp + ms) // tr,),
        in_specs=[pspec, sspec, pl.BlockSpec((1, d), lambda i: (0, 0))],
        out_specs=pl.BlockSpec((tr, d), lambda i: (i, 0)),
        out_shape=jax.ShapeDtypeStruct((mp + ms, d), BF16),
        compiler_params=pltpu.CompilerParams(
            dimension_semantics=("arbitrary",), vmem_limit_bytes=_vmem_limit(2 * 3 * tr * d * 4)),
        name="prenorm",
    )(xs[0], xs[1], w.reshape(1, d))


def _postnorm_kernel(*refs, nbp, n_in, last):
    i = pl.program_id(0)
    o_ref, w_ref = refs[n_in], refs[n_in + 1]
    x = refs[0][...] if n_in == 1 else jnp.where(i < nbp, refs[0][...], refs[1][...])
    y = x + _rms(o_ref[...], w_ref[...])
    if last:
        yp_ref, ys_ref = refs[n_in + 2:]

        @pl.when(i < nbp)
        def _():
            yp_ref[...] = y

        @pl.when(i >= nbp)
        def _():
            ys_ref[...] = y
    else:
        wn_ref, y_ref, z_ref = refs[n_in + 2:]
        y_ref[...] = y
        z_ref[...] = _rms(y, wn_ref[...]).astype(z_ref.dtype)


def _postnorm(xs, out, w, mp, ms, w_next):
    d = w.shape[0]
    tr, nbp, pspec, sspec = _group_specs(mp, ms, d)
    full = pl.BlockSpec((tr, d), lambda i: (i, 0))
    vec = pl.BlockSpec((1, d), lambda i: (0, 0))
    in_specs = ([pspec, sspec] if len(xs) == 2 else [full]) + [full, vec]
    args = [*xs, out, w.reshape(1, d)]
    if w_next is None:
        out_specs = [pspec, sspec]
        out_shape = [jax.ShapeDtypeStruct((mp, d), F32), jax.ShapeDtypeStruct((ms, d), F32)]
    else:
        in_specs.append(vec)
        args.append(w_next.reshape(1, d))
        out_specs = [full, full]
        out_shape = [jax.ShapeDtypeStruct((mp + ms, d), F32), jax.ShapeDtypeStruct((mp + ms, d), BF16)]
    return pl.pallas_call(
        functools.partial(_postnorm_kernel, nbp=nbp, n_in=len(xs), last=w_next is None),
        grid=((mp + ms) // tr,),
        in_specs=in_specs, out_specs=out_specs, out_shape=out_shape,
        compiler_params=pltpu.CompilerParams(
            dimension_semantics=("arbitrary",),
            vmem_limit_bytes=_vmem_limit(2 * (len(in_specs) + len(out_specs)) * tr * d * 4)),
        name="postnorm",
    )(*args)


def _mm_kernel(a_ref, w_ref, o_ref):
    o_ref[...] = _dot(a_ref[...], w_ref[...].astype(BF16))


def _mm_t_kernel(a_ref, wt_ref, o_ref):
    o_ref[...] = _dot_nt(a_ref[...], wt_ref[0].astype(BF16))


BIG_TM = (2112, 1408, 1056, 1024, 512, 320, 256, 128, 64, 32, 16)


def _matmul(a, w, layer, name):
    m, k = a.shape
    n = w.shape[2]
    tm = _pick(m, BIG_TM)
    tn = _pick(n, (512, 256, 128))
    need = tm * k * 2 + 2 * k * tn * 4 + k * tn * 2 + 3 * tm * tn * 4
    return pl.pallas_call(
        _mm_kernel,
        grid=(m // tm, n // tn),
        in_specs=[pl.BlockSpec((tm, k), lambda i, j: (i, 0), pipeline_mode=pl.Buffered(1)),
                  pl.BlockSpec((None, k, tn), lambda i, j: (layer, 0, j))],
        out_specs=pl.BlockSpec((tm, tn), lambda i, j: (i, j)),
        out_shape=jax.ShapeDtypeStruct((m, n), F32),
        compiler_params=pltpu.CompilerParams(
            dimension_semantics=_ARB2, vmem_limit_bytes=_vmem_limit(need)),
        name=name,
    )(a, w)


def _matmul_t(a, wt, layer, n_out, skip_start, skip, name):
    m, k = a.shape
    tm = _pick(m, BIG_TM)
    tn = next(c for c in (512, 256, 128, 64, 32, 16, 8)
              if n_out % c == 0 and skip_start % c == 0 and (c <= n_out // 2 or c == n_out))
    assert skip % SUBLANES == 0

    def w_index(i, j):
        row = j * tn + jnp.where(j * tn >= skip_start, skip, 0)
        return layer, pl.multiple_of(row, SUBLANES), 0

    need = tm * k * 2 + 2 * k * tn * 4 + k * tn * 2 + 3 * tm * tn * 4
    return pl.pallas_call(
        _mm_t_kernel,
        grid=(m // tm, n_out // tn),
        in_specs=[pl.BlockSpec((tm, k), lambda i, j: (i, 0), pipeline_mode=pl.Buffered(1)),
                  pl.BlockSpec((pl.Element(1), pl.Element(tn), pl.Element(k)), w_index)],
        out_specs=pl.BlockSpec((tm, tn), lambda i, j: (i, j)),
        out_shape=jax.ShapeDtypeStruct((m, n_out), F32),
        compiler_params=pltpu.CompilerParams(
            dimension_semantics=_ARB2, vmem_limit_bytes=_vmem_limit(need)),
        name=name,
    )(a, wt)


def _tile_row_index(n, d):
    return lax.broadcasted_iota(jnp.int32, (n, d), 0) & (SUBLANES - 1)


def _tile_roll(x, s):
    n, d = x.shape
    return pltpu.roll(x.reshape(n // SUBLANES, SUBLANES, d), s, 1).reshape(n, d)


def _tile_cumsum(g, rowi):
    c = g
    s = 1
    while s < SUBLANES:
        c = c + jnp.where(rowi >= s, _tile_roll(c, s), 0.0)
        s *= 2
    return c


def _tile_row_bcast(x, r):
    n, d = x.shape
    return jnp.concatenate(
        [jnp.broadcast_to(x[SUBLANES * i + r:SUBLANES * i + r + 1, :], (SUBLANES, d))
         for i in range(n // SUBLANES)], axis=0)


def _chunk_prepare(q, k, v, g, g_min):
    L, dk = q.shape
    nt = L // SUBLANES
    rowk = _tile_row_index(L, dk)

    c = _tile_cumsum(g, rowk)
    tot = [c[SUBLANES * i + SUBLANES - 1:SUBLANES * (i + 1), :] for i in range(nt)]
    qt = q * jnp.exp(c)
    knew = k * jnp.exp(_tile_row_bcast(c, SUBLANES - 1) - c)

    row = lax.broadcasted_iota(jnp.int32, (L, L), 0)
    col = lax.broadcasted_iota(jnp.int32, (L, L), 1)
    offs = jnp.where((row >> 3) == (col >> 3), row - col, -1)
    half = SUBLANES // 2
    if g_min is not None and -g_min * half < 80.0:
        cm = c - _tile_row_bcast(c, half - 1)
        inner = _dot_nt((q * jnp.exp(cm)).astype(BF16), (k * jnp.exp(-cm)).astype(BF16))
        band = jnp.where(offs >= 0, inner, 0.0)
    else:
        eg = jnp.exp(g)
        w = k
        band = jnp.where(offs == 0, jnp.sum(q * w, axis=1, keepdims=True), 0.0)
        for d in range(1, SUBLANES):
            w = _tile_roll(w, 1) * eg
            band = jnp.where(offs == d, jnp.sum(q * w, axis=1, keepdims=True), band)

    tiles = []
    arows = [jnp.zeros((SUBLANES, L), F32)]
    qhat = [qt[0:SUBLANES]]
    er = None
    for i in range(nt):
        lo, hi = SUBLANES * i, SUBLANES * (i + 1)
        if i > 0:
            pad = jnp.zeros((L - lo, dk), F32)
            kh = jnp.concatenate(tiles + [pad], axis=0).astype(BF16)
            arows.append(_dot_nt(qt[lo:hi].astype(BF16), kh))
            qhat.append(qt[lo:hi] * er)
        dcy = jnp.exp(tot[i])
        tiles = [t * dcy for t in tiles] + [knew[lo:hi]]
        er = dcy if er is None else er * dcy

    khat = jnp.concatenate(tiles, axis=0).astype(BF16)
    qh = jnp.concatenate(qhat, axis=0).astype(BF16)
    scores = (jnp.concatenate(arows, axis=0) + band).astype(BF16)
    return scores, v.astype(BF16), qh, khat, er


def _chunk_finish(scores, vb, qh, khat, er, st):
    o = _dot(scores, vb) + _dot_nt(qh, st.astype(BF16))
    return o, st * er + _dot_tn(vb, khat)


class _ChunkScratch:
    def __init__(self, refs):
        self.sc, self.vb, self.qh, self.kh, self.er = refs

    @staticmethod
    def shapes(nh, chunk, dk, dv):
        return [pltpu.VMEM((nh, chunk, chunk), BF16), pltpu.VMEM((nh, chunk, dv), BF16),
                pltpu.VMEM((nh, chunk, dk), BF16), pltpu.VMEM((nh, chunk, dk), BF16),
                pltpu.VMEM((nh, SUBLANES, dk), F32)]

    def put(self, h, scores, vb, qh, khat, er):
        self.sc[h] = scores
        self.vb[h] = vb
        self.qh[h] = qh
        self.kh[h] = khat
        self.er[h] = jnp.broadcast_to(er, self.er.shape[1:])

    def get(self, h):
        return self.sc[h], self.vb[h], self.qh[h], self.kh[h], self.er[h][0:1, :]


def _pipelined_chunks(n, prepare, finish):
    prepare(0)

    def body(ci, carry):
        finish(ci - 1)
        prepare(ci)
        return carry

    lax.fori_loop(1, n, body, 0)
    finish(n - 1)


def _sequential_chunks(n, prepare, finish):
    def body(ci, carry):
        prepare(ci)
        finish(ci)
        return carry

    lax.fori_loop(0, n, body, 0)


def _head_norm_gate(o, w, gate):
    on = o * lax.rsqrt(jnp.mean(o * o, axis=-1, keepdims=True) + EPS)
    return on * w * _silu(gate)


def _chunk_rows(ci, chunk):
    start = ci * chunk
    return pl.ds(start if isinstance(start, int) else pl.multiple_of(start, chunk), chunk)


def _hgrn_kernel(q_ref, f_ref, i_ref, gate_ref, lb_ref, nw_ref, s0_ref,
                 y_ref, sout_ref, st_ref, *scratch, chunk):
    t = pl.program_id(2)
    tb = q_ref.shape[0]
    nh, dk, dv = s0_ref.shape
    hand = _ChunkScratch(scratch)
    g_min = math.log(F_FLOOR) - 1e-3

    @pl.when(t == 0)
    def _():
        for h in range(nh):
            st_ref[h] = s0_ref[h].T

    def prepare(ci):
        rows = _chunk_rows(ci, chunk)
        for h in range(nh):
            cols = slice(h * dk, (h + 1) * dk)
            lb = lb_ref[:, cols]
            f = lb + (1.0 - lb) * _sigmoid(f_ref[rows, cols])
            g = jnp.log(jnp.maximum(f, F_FLOOR))
            hand.put(h, *_chunk_prepare(_silu(q_ref[rows, cols]), 1.0 - f, i_ref[rows, cols], g, g_min))

    def finish(ci):
        rows = _chunk_rows(ci, chunk)
        for h in range(nh):
            cols = slice(h * dk, (h + 1) * dk)
            o, st_ref[h] = _chunk_finish(*hand.get(h), st_ref[h])
            y = _head_norm_gate(o, nw_ref[:, cols], gate_ref[rows, cols])
            y_ref[rows, cols] = y.astype(y_ref.dtype)

    _pipelined_chunks(tb // chunk, prepare, finish)

    @pl.when(t == pl.num_programs(2) - 1)
    def _():
        for h in range(nh):
            sout_ref[h] = st_ref[h].T


def _hgrn(proj, row0, nb, t_len, lb, hg_norm, s0, col0, y_prev):
    _, nh, dk, dv = s0.shape
    w = nh * dk
    hpb = _pick(nh, (4, 2, 1))
    bw = hpb * dk
    tb = _pick(t_len, (1024, 512, 256, 128, 64, 32))
    chunk = min(LA_CHUNK, tb)
    nt = t_len // tb
    rb0 = row0 // tb

    def col(g):
        return pl.BlockSpec((tb, bw), lambda b, h, t: (rb0 + b * nt + t, (col0 + g * w) // bw + h))

    vec = pl.BlockSpec((1, bw), lambda b, h, t: (0, h))
    st = pl.BlockSpec((None, hpb, dk, dv), lambda b, h, t: (b, h, 0, 0))
    kern = functools.partial(_hgrn_kernel, chunk=chunk)
    args = [proj, proj, proj, proj, lb.reshape(1, w), hg_norm.reshape(1, w), s0]
    y, s_new = pl.pallas_call(
        kern if y_prev is None else _without_ref(kern, len(args)),
        grid=(nb, nh // hpb, nt),
        in_specs=[col(0), col(1), col(2), col(3), vec, vec, st] + _alias_spec(y_prev),
        out_specs=[pl.BlockSpec((tb, bw), lambda b, h, t: (rb0 + b * nt + t, h)), st],
        out_shape=[jax.ShapeDtypeStruct((proj.shape[0], w), BF16),
                   jax.ShapeDtypeStruct(s0.shape, F32)],
        scratch_shapes=[pltpu.VMEM((hpb, dv, dk), F32)] + _ChunkScratch.shapes(hpb, chunk, dk, dv),
        input_output_aliases=_alias_map(y_prev, len(args)),
        compiler_params=pltpu.CompilerParams(dimension_semantics=_ARB3),
        name="hgrn2",
    )(*args, *_alias_arg(y_prev))
    return y, s_new


def _gla_kernel(q_ref, k_ref, v_ref, gate_ref, lr_ref, w2_ref, b2_ref, nw_ref, s0_ref,
                y_ref, sout_ref, st_ref, *scratch, chunk, scale):
    t = pl.program_id(2)
    tb = q_ref.shape[0]
    nh, dk, dv = s0_ref.shape
    hand = _ChunkScratch(scratch)

    @pl.when(t == 0)
    def _():
        for h in range(nh):
            st_ref[h] = s0_ref[h].T

    def prepare(ci):
        rows = _chunk_rows(ci, chunk)
        logits = _dot(lr_ref[rows, :].astype(BF16), w2_ref[...].astype(BF16)) + b2_ref[...]
        g = (jnp.minimum(logits, 0.0) - jnp.log1p(jnp.exp(-jnp.abs(logits)))) / GLA_TAU
        for h in range(nh):
            kc, vc = slice(h * dk, (h + 1) * dk), slice(h * dv, (h + 1) * dv)
            hand.put(h, *_chunk_prepare(q_ref[rows, kc] * scale, k_ref[rows, kc], v_ref[rows, vc],
                                        g[:, kc], None))

    def finish(ci):
        rows = _chunk_rows(ci, chunk)
        for h in range(nh):
            vc = slice(h * dv, (h + 1) * dv)
            o, st_ref[h] = _chunk_finish(*hand.get(h), st_ref[h])
            y_ref[rows, vc] = _head_norm_gate(o, nw_ref[:, vc], gate_ref[rows, vc]).astype(y_ref.dtype)

    _sequential_chunks(tb // chunk, prepare, finish)

    @pl.when(t == pl.num_programs(2) - 1)
    def _():
        for h in range(nh):
            sout_ref[h] = st_ref[h].T


def _gla(proj, lr, row0, nb, t_len, w2, b2, gla_norm, s0, col_q, col_k, col_v, col_g, y_prev):
    _, nh, dk, dv = s0.shape
    rank = lr.shape[1]
    hpb = _pick(nh, (2, 1))
    kw, vw = hpb * dk, hpb * dv
    tb = _pick(t_len, (512, 256, 128, 64, 32))
    chunk = min(LA_CHUNK, tb)
    nt = t_len // tb
    rb0 = row0 // tb

    def col(c0, wd):
        return pl.BlockSpec((tb, wd), lambda b, h, t: (rb0 + b * nt + t, c0 // wd + h))

    st = pl.BlockSpec((None, hpb, dk, dv), lambda b, h, t: (b, h, 0, 0))
    kern = functools.partial(_gla_kernel, chunk=chunk, scale=float(dk) ** -0.5)
    args = [proj, proj, proj, proj, lr, w2, b2.reshape(1, nh * dk),
            gla_norm.reshape(1, nh * dv), s0]
    y, s_new = pl.pallas_call(
        kern if y_prev is None else _without_ref(kern, len(args)),
        grid=(nb, nh // hpb, nt),
        in_specs=[col(col_q, kw), col(col_k, kw), col(col_v, vw), col(col_g, vw),
                  pl.BlockSpec((tb, rank), lambda b, h, t: (rb0 + b * nt + t, 0)),
                  pl.BlockSpec((rank, kw), lambda b, h, t: (0, h)),
                  pl.BlockSpec((1, kw), lambda b, h, t: (0, h)),
                  pl.BlockSpec((1, vw), lambda b, h, t: (0, h)),
                  st] + _alias_spec(y_prev),
        out_specs=[pl.BlockSpec((tb, vw), lambda b, h, t: (rb0 + b * nt + t, h)), st],
        out_shape=[jax.ShapeDtypeStruct((proj.shape[0], nh * dv), BF16),
                   jax.ShapeDtypeStruct(s0.shape, F32)],
        scratch_shapes=[pltpu.VMEM((hpb, dv, dk), F32)] + _ChunkScratch.shapes(hpb, chunk, dk, dv),
        input_output_aliases=_alias_map(y_prev, len(args)),
        compiler_params=pltpu.CompilerParams(dimension_semantics=_ARB3),
        name="gla",
    )(*args, *_alias_arg(y_prev))
    return y, s_new


def _lru_kernel(x_ref, gate_ref, cs_ref, h0_ref, cw_ref, cb_ref, wa_ref, ba_ref, wx_ref, bx_ref,
                lam_ref, y_ref, hout_ref, cout_ref, hist_ref, hcar_ref, a_ref, u_ref, h_ref):
    t = pl.program_id(2)
    tb, wd = x_ref.shape
    ncw = cw_ref.shape[0]
    nblk, bd, _ = wa_ref.shape

    @pl.when(t == 0)
    def _():
        hist_ref[...] = jnp.zeros_like(hist_ref)
        hist_ref[SUBLANES - (ncw - 1):SUBLANES, :] = cs_ref[...]
        hcar_ref[...] = jnp.broadcast_to(h0_ref[...], (SUBLANES, wd))

    x = x_ref[...]
    xe = jnp.concatenate([hist_ref[...], x], axis=0)
    cw = cw_ref[...]
    xc = cb_ref[...] + x * cw[ncw - 1:ncw, :]
    for s in range(1, ncw):
        xc = xc + pltpu.roll(xe, s, 0)[SUBLANES:, :] * cw[ncw - 1 - s:ncw - s, :]
    hist_ref[...] = x[tb - SUBLANES:tb, :]

    xb = xc.astype(BF16)
    ra, ri = [], []
    for j in range(nblk):
        xj = xb[:, j * bd:(j + 1) * bd]
        ra.append(_dot(xj, wa_ref[j].astype(BF16)))
        ri.append(_dot(xj, wx_ref[j].astype(BF16)))
    r = _sigmoid(jnp.concatenate(ra, axis=1) + ba_ref[...])
    gi = _sigmoid(jnp.concatenate(ri, axis=1) + bx_ref[...])
    lam = lam_ref[...]
    softplus_neg = jnp.maximum(-lam, 0.0) + jnp.log1p(jnp.exp(-jnp.abs(lam)))
    log_a = -LRU_C * r * softplus_neg
    a = jnp.exp(log_a)
    x2 = 2.0 * log_a
    e2 = jnp.exp(x2)
    small = jnp.where(e2 == 1.0, -x2, (1.0 - e2) * x2 / jnp.log(jnp.where(e2 > 0.25, e2, 0.5)))
    one_m_a2 = jnp.where(x2 < -0.5, 1.0 - e2, small)
    u = jnp.sqrt(jnp.maximum(one_m_a2, 0.0)) * (gi * xc)

    rowi = _tile_row_index(tb, wd)
    s = 1
    while s < SUBLANES:
        ok = rowi >= s
        u = u + jnp.where(ok, a * _tile_roll(u, s), 0.0)
        a = a * jnp.where(ok, _tile_roll(a, s), 1.0)
        s *= 2
    a_ref[...] = a
    u_ref[...] = u

    def body(i, hp):
        rows = pl.ds(pl.multiple_of(i * SUBLANES, SUBLANES), SUBLANES)
        h = u_ref[rows, :] + a_ref[rows, :] * hp
        h_ref[rows, :] = h
        return jnp.broadcast_to(h[SUBLANES - 1:SUBLANES, :], (SUBLANES, wd))

    hp = lax.fori_loop(0, tb // SUBLANES, body, hcar_ref[...], unroll=4)
    hcar_ref[...] = hp
    y_ref[...] = (h_ref[...] * _silu(gate_ref[...])).astype(y_ref.dtype)

    @pl.when(t == pl.num_programs(2) - 1)
    def _():
        hout_ref[...] = hp[0:1, :]
        cout_ref[...] = x[tb - (ncw - 1):tb, :]


def _lru(proj, row0, nb, t_len, conv_state, h0, conv_w, conv_b, wa, ba, wx, bx, lam, col_x, col_g, y_prev):
    nblk, bd, _ = wa.shape
    w = nblk * bd
    ncw = conv_w.shape[0]
    bps = _pick(nblk, (4, 2, 1))
    bw = bps * bd
    tb = _pick(t_len, (256, 128, 64, 32))
    nt = t_len // tb
    rb0 = row0 // tb

    def col(c0):
        return pl.BlockSpec((tb, bw), lambda b, h, t: (rb0 + b * nt + t, c0 // bw + h))

    vec = pl.BlockSpec((1, bw), lambda b, h, t: (0, h))
    blk = pl.BlockSpec((bps, bd, bd), lambda b, h, t: (h, 0, 0))
    hspec = pl.BlockSpec((None, 1, bw), lambda b, h, t: (b, 0, h))
    cspec = pl.BlockSpec((None, ncw - 1, bw), lambda b, h, t: (b, 0, h))
    args = [proj, proj, conv_state, h0.reshape(nb, 1, w), conv_w, conv_b.reshape(1, w),
            wa, ba.reshape(1, w), wx, bx.reshape(1, w), lam.reshape(1, w)]
    y, h_new, c_new = pl.pallas_call(
        _lru_kernel if y_prev is None else _without_ref(_lru_kernel, len(args)),
        grid=(nb, nblk // bps, nt),
        in_specs=[col(col_x), col(col_g), cspec, hspec,
                  pl.BlockSpec((ncw, bw), lambda b, h, t: (0, h)), vec,
                  blk, vec, blk, vec, vec] + _alias_spec(y_prev),
        out_specs=[pl.BlockSpec((tb, bw), lambda b, h, t: (rb0 + b * nt + t, h)), hspec, cspec],
        out_shape=[jax.ShapeDtypeStruct((proj.shape[0], w), BF16),
                   jax.ShapeDtypeStruct((nb, 1, w), F32),
                   jax.ShapeDtypeStruct((nb, ncw - 1, w), F32)],
        scratch_shapes=[pltpu.VMEM((SUBLANES, bw), F32), pltpu.VMEM((SUBLANES, bw), F32),
                        pltpu.VMEM((tb, bw), F32), pltpu.VMEM((tb, bw), F32),
                        pltpu.VMEM((tb, bw), F32)],
        input_output_aliases=_alias_map(y_prev, len(args)),
        compiler_params=pltpu.CompilerParams(dimension_semantics=_ARB3),
        name="rglru",
    )(*args, *_alias_arg(y_prev))
    return y, h_new.reshape(nb, w), c_new


def _merge_kernel(ya_ref, yb_ref, yc_ref, w_ref, m0_ref, m1_ref, m2_ref, o_ref):
    acc = _sigmoid(m0_ref[...]) * _dot(ya_ref[...], w_ref[0].astype(BF16))
    acc = acc + _sigmoid(m1_ref[...]) * _dot(yb_ref[...], w_ref[1].astype(BF16))
    acc = acc + _sigmoid(m2_ref[...]) * _dot(yc_ref[...], w_ref[2].astype(BF16))
    o_ref[...] = acc.astype(o_ref.dtype)


def _merge(ya, yb, yc, wbr, layer, proj, col_m):
    m, bw = ya.shape
    _, nbr, _, d = wbr.shape
    tm = _pick(m, (1056, 1024, 512, 320, 256, 128, 64, 32, 16))
    tn = next(c for c in (256, 128) if d % c == 0 and col_m % c == 0)
    yspec = pl.BlockSpec((tm, bw), lambda i, j: (i, 0), pipeline_mode=pl.Buffered(1))

    def mg(n):
        return pl.BlockSpec((tm, tn), lambda i, j: (i, (col_m + n * d) // tn + j))

    need = (3 * tm * bw * 2 + 2 * nbr * bw * tn * 4 + nbr * bw * tn * 2
            + 2 * 3 * tm * tn * 4 + 6 * tm * tn * 4)
    return pl.pallas_call(
        _merge_kernel,
        grid=(m // tm, d // tn),
        in_specs=[yspec, yspec, yspec,
                  pl.BlockSpec((None, nbr, bw, tn), lambda i, j: (layer, 0, 0, j)),
                  mg(0), mg(1), mg(2)],
        out_specs=pl.BlockSpec((tm, tn), lambda i, j: (i, j)),
        out_shape=jax.ShapeDtypeStruct((m, d), BF16),
        compiler_params=pltpu.CompilerParams(
            dimension_semantics=_ARB2, vmem_limit_bytes=_vmem_limit(need)),
        name="merge",
    )(ya, yb, yc, wbr, proj, proj, proj)


def kernel(x_prompt, x_sample, state_hgrn, state_lru_h, state_lru_conv, state_gla, norm_pre, norm_post, w_in, hg_lb_logits, hg_norm, lru_conv_w, lru_conv_b, lru_wa, lru_ba, lru_wx, lru_bx, lru_lambda, gla_w2, gla_b2, gla_norm, w_branch, w_out):
    bp, tp, d = x_prompt.shape
    bs, ts, _ = x_sample.shape
    depth, _, in_cols = w_in.shape
    _, _, hg_h, hg_dk, hg_dv = state_hgrn.shape
    hg_w = hg_h * hg_dk
    lru_w = state_lru_h.shape[-1]
    _, _, gl_h, gl_dk, gl_dv = state_gla.shape
    gl_kw, gl_vw = gl_h * gl_dk, gl_h * gl_dv
    rank = gla_w2.shape[1]
    dt = x_prompt.dtype

    c_hg = 0
    c_lx = 4 * hg_w
    c_lg = c_lx + lru_w
    c_q = c_lg + lru_w
    c_k = c_q + gl_kw
    c_v = c_k + gl_kw
    c_lr = c_v + gl_vw
    c_cg = c_lr
    c_m = c_cg + gl_vw
    n_main = in_cols - rank
    assert c_m + 3 * d == n_main

    lb_sm = jax.nn.softmax(hg_lb_logits.astype(F32), axis=0)
    lb_all = jnp.cumsum(lb_sm, axis=0) - lb_sm[0:1]
    w_in_t = jnp.transpose(w_in, (0, 2, 1))

    mp, ms = bp * tp, bs * ts
    xs = (x_prompt.reshape(mp, d), x_sample.reshape(ms, d))

    zeros = lambda shape: jnp.zeros(shape, dt)
    outs = {k: [] for k in ("hg_p", "hg_s", "lh_p", "lh_s", "lc_p", "lc_s", "gl_p", "gl_s")}
    z = _prenorm(xs, norm_pre[0])
    for l in range(depth):
        proj = _matmul_t(z, w_in_t, l, n_main, c_lr, rank, "in_proj")
        lr = _matmul_t(z, w_in_t, l, rank, 0, c_lr, "lr_proj")

        groups = (
            ("p", 0, bp, tp, zeros((bp,) + state_hgrn.shape[2:]), zeros((bp, lru_w)),
             zeros((bp,) + state_lru_conv.shape[2:]), zeros((bp,) + state_gla.shape[2:])),
            ("s", mp, bs, ts, state_hgrn[l], state_lru_h[l], state_lru_conv[l], state_gla[l]),
        )
        ya = yb = yc = None
        for tag, row0, nb, t_len, s_hg, s_lh, s_lc, s_gl in groups:
            ya, s = _hgrn(proj, row0, nb, t_len, lb_all[l], hg_norm[l], s_hg, c_hg, ya)
            outs["hg_" + tag].append(s)
            yb, hn, cn = _lru(proj, row0, nb, t_len, s_lc, s_lh, lru_conv_w[l], lru_conv_b[l],
                              lru_wa[l], lru_ba[l], lru_wx[l], lru_bx[l], lru_lambda[l], c_lx, c_lg, yb)
            outs["lh_" + tag].append(hn); outs["lc_" + tag].append(cn)
            yc, s = _gla(proj, lr, row0, nb, t_len, gla_w2[l], gla_b2[l], gla_norm[l], s_gl,
                         c_q, c_k, c_v, c_cg, yc)
            outs["gl_" + tag].append(s)

        merged = _merge(ya, yb, yc, w_branch, l, proj, c_m)
        out = _matmul(merged, w_out, l, "out_proj")
        if l < depth - 1:
            x, z = _postnorm(xs, out, norm_post[l], mp, ms, norm_pre[l + 1])
            xs = (x,)
        else:
            xs = _postnorm(xs, out, norm_post[l], mp, ms, None)

    yp = xs[0].reshape(bp, tp, d)
    ys = xs[1].reshape(bs, ts, d)
    st = lambda k: jnp.stack(outs[k])
    return (yp, ys, st("hg_p"), st("hg_s"), st("lh_p"), st("lh_s"),
            st("lc_p"), st("lc_s"), st("gl_p"), st("gl_s"))
```

```python
import functools
import math

import jax
import jax.numpy as jnp
from jax import lax
from jax.experimental import pallas as pl
from jax.experimental.pallas import tpu as pltpu

F32 = jnp.float32
BF16 = jnp.bfloat16

EPS = 1e-6
F_FLOOR = 1e-6
LRU_C = 8.0
GLA_TAU = 16.0
SUBLANES = 8
LANES = 128
LA_CHUNK = 64
V7X_VMEM_LIMIT = 60000 * 1024


def _vmem_limit(nbytes):
    return int(min(V7X_VMEM_LIMIT, max(32 * 1024 * 1024, nbytes * 5 // 4 + (4 << 20))))


def _pick(n, candidates):
    for c in candidates:
        if n % c == 0:
            return c
    return n


def _dot(a, b):
    return jnp.dot(a, b, preferred_element_type=F32)


def _dot_nt(a, b):
    return lax.dot_general(a, b, (((1,), (1,)), ((), ())), preferred_element_type=F32)


def _dot_tn(a, b):
    return lax.dot_general(a, b, (((0,), (0,)), ((), ())), preferred_element_type=F32)


def _sigmoid(x):
    return jax.nn.sigmoid(x)


def _silu(x):
    return x * jax.nn.sigmoid(x)


def _without_ref(kernel, idx):
    def wrapped(*refs):
        return kernel(*refs[:idx], *refs[idx + 1:])
    return wrapped


def _alias_spec(y_prev):
    return [] if y_prev is None else [pl.BlockSpec(memory_space=pl.ANY)]


def _alias_arg(y_prev):
    return [] if y_prev is None else [y_prev]


def _alias_map(y_prev, idx):
    return {} if y_prev is None else {idx: 0}


_ARB2 = ("arbitrary", "arbitrary")
_ARB3 = ("arbitrary", "arbitrary", "arbitrary")


def _rms(x, w):
    return x * lax.rsqrt(jnp.mean(x * x, axis=-1, keepdims=True) + EPS) * w


def _group_specs(mp, ms, d):
    tr = next(c for c in (256, 128, 64, 32, 16, 8) if mp % c == 0 and ms % c == 0)
    nbp = mp // tr
    pspec = pl.BlockSpec((tr, d), lambda i: (jnp.minimum(i, nbp - 1), 0))
    sspec = pl.BlockSpec((tr, d), lambda i: (jnp.maximum(i - nbp, 0), 0))
    return tr, nbp, pspec, sspec


def _prenorm2_kernel(xp_ref, xs_ref, w_ref, z_ref, *, nbp):
    x = jnp.where(pl.program_id(0) < nbp, xp_ref[...], xs_ref[...])
    z_ref[...] = _rms(x, w_ref[...]).astype(z_ref.dtype)


def _prenorm(xs, w):
    d = w.shape[0]
    mp, ms = xs[0].shape[0], xs[1].shape[0]
    tr, nbp, pspec, sspec = _group_specs(mp, ms, d)
    return pl.pallas_call(
        functools.partial(_prenorm2_kernel, nbp=nbp),
        grid=((mp + ms) // tr,),
        in_specs=[pspec, sspec, pl.BlockSpec((1, d), lambda i: (0, 0))],
        out_specs=pl.BlockSpec((tr, d), lambda i: (i, 0)),
        out_shape=jax.ShapeDtypeStruct((mp + ms, d), BF16),
        compiler_params=pltpu.CompilerParams(
            dimension_semantics=("arbitrary",), vmem_limit_bytes=_vmem_limit(2 * 3 * tr * d * 4)),
        name="prenorm",
    )(xs[0], xs[1], w.reshape(1, d))


def _postnorm_kernel(*refs, nbp, n_in, last):
    i = pl.program_id(0)
    o_ref, w_ref = refs[n_in], refs[n_in + 1]
    x = refs[0][...] if n_in == 1 else jnp.where(i < nbp, refs[0][...], refs[1][...])
    y = x + _rms(o_ref[...], w_ref[...])
    if last:
        yp_ref, ys_ref = refs[n_in + 2:]

        @pl.when(i < nbp)
        def _():
            yp_ref[...] = y

        @pl.when(i >= nbp)
        def _():
            ys_ref[...] = y
    else:
        wn_ref, y_ref, z_ref = refs[n_in + 2:]
        y_ref[...] = y
        z_ref[...] = _rms(y, wn_ref[...]).astype(z_ref.dtype)


def _postnorm(xs, out, w, mp, ms, w_next):
    d = w.shape[0]
    tr, nbp, pspec, sspec = _group_specs(mp, ms, d)
    full = pl.BlockSpec((tr, d), lambda i: (i, 0))
    vec = pl.BlockSpec((1, d), lambda i: (0, 0))
    in_specs = ([pspec, sspec] if len(xs) == 2 else [full]) + [full, vec]
    args = [*xs, out, w.reshape(1, d)]
    if w_next is None:
        out_specs = [pspec, sspec]
        out_shape = [jax.ShapeDtypeStruct((mp, d), F32), jax.ShapeDtypeStruct((ms, d), F32)]
    else:
        in_specs.append(vec)
        args.append(w_next.reshape(1, d))
        out_specs = [full, full]
        out_shape = [jax.ShapeDtypeStruct((mp + ms, d), F32), jax.ShapeDtypeStruct((mp + ms, d), BF16)]
    return pl.pallas_call(
        functools.partial(_postnorm_kernel, nbp=nbp, n_in=len(xs), last=w_next is None),
        grid=((mp + ms) // tr,),
        in_specs=in_specs, out_specs=out_specs, out_shape=out_shape,
        compiler_params=pltpu.CompilerParams(
            dimension_semantics=("arbitrary",),
            vmem_limit_bytes=_vmem_limit(2 * (len(in_specs) + len(out_specs)) * tr * d * 4)),
        name="postnorm",
    )(*args)


def _mm_kernel(a_ref, w_ref, o_ref):
    o_ref[...] = _dot(a_ref[...], w_ref[...].astype(BF16))


def _mm_t_kernel(a_ref, wt_ref, o_ref):
    o_ref[...] = _dot_nt(a_ref[...], wt_ref[0].astype(BF16))


BIG_TM = (2112, 1408, 1056, 1024, 512, 320, 256, 128, 64, 32, 16)


def _matmul(a, w, layer, name):
    m, k = a.shape
    n = w.shape[2]
    tm = _pick(m, BIG_TM)
    tn = _pick(n, (512, 256, 128))
    need = tm * k * 2 + 2 * k * tn * 4 + k * tn * 2 + 3 * tm * tn * 4
    return pl.pallas_call(
        _mm_kernel,
        grid=(m // tm, n // tn),
        in_specs=[pl.BlockSpec((tm, k), lambda i, j: (i, 0), pipeline_mode=pl.Buffered(1)),
                  pl.BlockSpec((None, k, tn), lambda i, j: (layer, 0, j))],
        out_specs=pl.BlockSpec((tm, tn), lambda i, j: (i, j)),
        out_shape=jax.ShapeDtypeStruct((m, n), F32),
        compiler_params=pltpu.CompilerParams(
            dimension_semantics=_ARB2, vmem_limit_bytes=_vmem_limit(need)),
        name=name,
    )(a, w)


def _matmul_t(a, wt, layer, n_out, skip_start, skip, name):
    m, k = a.shape
    tm = _pick(m, BIG_TM)
    tn = next(c for c in (512, 256, 128, 64, 32, 16, 8)
              if n_out % c == 0 and skip_start % c == 0 and (c <= n_out // 2 or c == n_out))
    assert skip % SUBLANES == 0

    def w_index(i, j):
        row = j * tn + jnp.where(j * tn >= skip_start, skip, 0)
        return layer, pl.multiple_of(row, SUBLANES), 0

    need = tm * k * 2 + 2 * k * tn * 4 + k * tn * 2 + 3 * tm * tn * 4
    return pl.pallas_call(
        _mm_t_kernel,
        grid=(m // tm, n_out // tn),
        in_specs=[pl.BlockSpec((tm, k), lambda i, j: (i, 0), pipeline_mode=pl.Buffered(1)),
                  pl.BlockSpec((pl.Element(1), pl.Element(tn), pl.Element(k)), w_index)],
        out_specs=pl.BlockSpec((tm, tn), lambda i, j: (i, j)),
        out_shape=jax.ShapeDtypeStruct((m, n_out), F32),
        compiler_params=pltpu.CompilerParams(
            dimension_semantics=_ARB2, vmem_limit_bytes=_vmem_limit(need)),
        name=name,
    )(a, wt)


def _tile_row_index(n, d):
    return lax.broadcasted_iota(jnp.int32, (n, d), 0) & (SUBLANES - 1)


def _tile_roll(x, s):
    n, d = x.shape
    return pltpu.roll(x.reshape(n // SUBLANES, SUBLANES, d), s, 1).reshape(n, d)


def _tile_cumsum(g, rowi):
    c = g
    s = 1
    while s < SUBLANES:
        c = c + jnp.where(rowi >= s, _tile_roll(c, s), 0.0)
        s *= 2
    return c


def _tile_row_bcast(x, r):
    n, d = x.shape
    return jnp.concatenate(
        [jnp.broadcast_to(x[SUBLANES * i + r:SUBLANES * i + r + 1, :], (SUBLANES, d))
         for i in range(n // SUBLANES)], axis=0)


def _chunk_prepare(q, k, v, g, g_min):
    L, dk = q.shape
    nt = L // SUBLANES
    rowk = _tile_row_index(L, dk)

    c = _tile_cumsum(g, rowk)
    tot = [c[SUBLANES * i + SUBLANES - 1:SUBLANES * (i + 1), :] for i in range(nt)]
    qt = q * jnp.exp(c)
    knew = k * jnp.exp(_tile_row_bcast(c, SUBLANES - 1) - c)

    row = lax.broadcasted_iota(jnp.int32, (L, L), 0)
    col = lax.broadcasted_iota(jnp.int32, (L, L), 1)
    offs = jnp.where((row >> 3) == (col >> 3), row - col, -1)
    half = SUBLANES // 2
    if g_min is not None and -g_min * half < 80.0:
        cm = c - _tile_row_bcast(c, half - 1)
        inner = _dot_nt((q * jnp.exp(cm)).astype(BF16), (k * jnp.exp(-cm)).astype(BF16))
        band = jnp.where(offs >= 0, inner, 0.0)
    else:
        eg = jnp.exp(g)
        w = k
        band = jnp.where(offs == 0, jnp.sum(q * w, axis=1, keepdims=True), 0.0)
        for d in range(1, SUBLANES):
            w = _tile_roll(w, 1) * eg
            band = jnp.where(offs == d, jnp.sum(q * w, axis=1, keepdims=True), band)

    tiles = []
    arows = [jnp.zeros((SUBLANES, L), F32)]
    qhat = [qt[0:SUBLANES]]
    er = None
    for i in range(nt):
        lo, hi = SUBLANES * i, SUBLANES * (i + 1)
        if i > 0:
            pad = jnp.zeros((L - lo, dk), F32)
            kh = jnp.concatenate(tiles + [pad], axis=0).astype(BF16)
            arows.append(_dot_nt(qt[lo:hi].astype(BF16), kh))
            qhat.append(qt[lo:hi] * er)
        dcy = jnp.exp(tot[i])
        tiles = [t * dcy for t in tiles] + [knew[lo:hi]]
        er = dcy if er is None else er * dcy

    khat = jnp.concatenate(tiles, axis=0).astype(BF16)
    qh = jnp.concatenate(qhat, axis=0).astype(BF16)
    scores = (jnp.concatenate(arows, axis=0) + band).astype(BF16)
    return scores, v.astype(BF16), qh, khat, er


def _chunk_finish(scores, vb, qh, khat, er, st):
    o = _dot(scores, vb) + _dot_nt(qh, st.astype(BF16))
    return o, st * er + _dot_tn(vb, khat)


class _ChunkScratch:
    def __init__(self, refs):
        self.sc, self.vb, self.qh, self.kh, self.er = refs

    @staticmethod
    def shapes(nh, chunk, dk, dv):
        return [pltpu.VMEM((nh, chunk, chunk), BF16), pltpu.VMEM((nh, chunk, dv), BF16),
                pltpu.VMEM((nh, chunk, dk), BF16), pltpu.VMEM((nh, chunk, dk), BF16),
                pltpu.VMEM((nh, SUBLANES, dk), F32)]

    def put(self, h, scores, vb, qh, khat, er):
        self.sc[h] = scores
        self.vb[h] = vb
        self.qh[h] = qh
        self.kh[h] = khat
        self.er[h] = jnp.broadcast_to(er, self.er.shape[1:])

    def get(self, h):
        return self.sc[h], self.vb[h], self.qh[h], self.kh[h], self.er[h][0:1, :]


def _pipelined_chunks(n, prepare, finish):
    prepare(0)

    def body(ci, carry):
        finish(ci - 1)
        prepare(ci)
        return carry

    lax.fori_loop(1, n, body, 0)
    finish(n - 1)


def _sequential_chunks(n, prepare, finish):
    def body(ci, carry):
        prepare(ci)
        finish(ci)
        return carry

    lax.fori_loop(0, n, body, 0)


def _head_norm_gate(o, w, gate):
    on = o * lax.rsqrt(jnp.mean(o * o, axis=-1, keepdims=True) + EPS)
    return on * w * _silu(gate)


def _chunk_rows(ci, chunk):
    start = ci * chunk
    return pl.ds(start if isinstance(start, int) else pl.multiple_of(start, chunk), chunk)


def _hgrn_kernel(q_ref, f_ref, i_ref, gate_ref, lb_ref, nw_ref, s0_ref,
                 y_ref, sout_ref, st_ref, *scratch, chunk):
    t = pl.program_id(2)
    tb = q_ref.shape[0]
    nh, dk, dv = s0_ref.shape
    hand = _ChunkScratch(scratch)
    g_min = math.log(F_FLOOR) - 1e-3

    @pl.when(t == 0)
    def _():
        for h in range(nh):
            st_ref[h] = s0_ref[h].T

    def prepare(ci):
        rows = _chunk_rows(ci, chunk)
        for h in range(nh):
            cols = slice(h * dk, (h + 1) * dk)
            lb = lb_ref[:, cols]
            f = lb + (1.0 - lb) * _sigmoid(f_ref[rows, cols])
            g = jnp.log(jnp.maximum(f, F_FLOOR))
            hand.put(h, *_chunk_prepare(_silu(q_ref[rows, cols]), 1.0 - f, i_ref[rows, cols], g, g_min))

    def finish(ci):
        rows = _chunk_rows(ci, chunk)
        for h in range(nh):
            cols = slice(h * dk, (h + 1) * dk)
            o, st_ref[h] = _chunk_finish(*hand.get(h), st_ref[h])
            y = _head_norm_gate(o, nw_ref[:, cols], gate_ref[rows, cols])
            y_ref[rows, cols] = y.astype(y_ref.dtype)

    _pipelined_chunks(tb // chunk, prepare, finish)

    @pl.when(t == pl.num_programs(2) - 1)
    def _():
        for h in range(nh):
            sout_ref[h] = st_ref[h].T


def _hgrn(proj, row0, nb, t_len, lb, hg_norm, s0, col0, y_prev):
    _, nh, dk, dv = s0.shape
    w = nh * dk
    hpb = nh if t_len <= LA_CHUNK else _pick(nh, (4, 2, 1))
    bw = hpb * dk
    tb = _pick(t_len, (1024, 512, 256, 128, 64, 32))
    chunk = min(LA_CHUNK, tb)
    nt = t_len // tb
    rb0 = row0 // tb

    def col(g):
        return pl.BlockSpec((tb, bw), lambda b, h, t: (rb0 + b * nt + t, (col0 + g * w) // bw + h))

    vec = pl.BlockSpec((1, bw), lambda b, h, t: (0, h))
    st = pl.BlockSpec((None, hpb, dk, dv), lambda b, h, t: (b, h, 0, 0))
    kern = functools.partial(_hgrn_kernel, chunk=chunk)
    args = [proj, proj, proj, proj, lb.reshape(1, w), hg_norm.reshape(1, w), s0]
    y, s_new = pl.pallas_call(
        kern if y_prev is None else _without_ref(kern, len(args)),
        grid=(nb, nh // hpb, nt),
        in_specs=[col(0), col(1), col(2), col(3), vec, vec, st] + _alias_spec(y_prev),
        out_specs=[pl.BlockSpec((tb, bw), lambda b, h, t: (rb0 + b * nt + t, h)), st],
        out_shape=[jax.ShapeDtypeStruct((proj.shape[0], w), BF16),
                   jax.ShapeDtypeStruct(s0.shape, F32)],
        scratch_shapes=[pltpu.VMEM((hpb, dv, dk), F32)] + _ChunkScratch.shapes(hpb, chunk, dk, dv),
        input_output_aliases=_alias_map(y_prev, len(args)),
        compiler_params=pltpu.CompilerParams(dimension_semantics=_ARB3),
        name="hgrn2",
    )(*args, *_alias_arg(y_prev))
    return y, s_new


def _gla_kernel(q_ref, k_ref, v_ref, gate_ref, lr_ref, w2_ref, b2_ref, nw_ref, s0_ref,
                y_ref, sout_ref, st_ref, *scratch, chunk, scale):
    t = pl.program_id(2)
    tb = q_ref.shape[0]
    nh, dk, dv = s0_ref.shape
    hand = _ChunkScratch(scratch)

    @pl.when(t == 0)
    def _():
        for h in range(nh):
            st_ref[h] = s0_ref[h].T

    def prepare(ci):
        rows = _chunk_rows(ci, chunk)
        logits = _dot(lr_ref[rows, :].astype(BF16), w2_ref[...].astype(BF16)) + b2_ref[...]
        g = (jnp.minimum(logits, 0.0) - jnp.log1p(jnp.exp(-jnp.abs(logits)))) / GLA_TAU
        for h in range(nh):
            kc, vc = slice(h * dk, (h + 1) * dk), slice(h * dv, (h + 1) * dv)
            hand.put(h, *_chunk_prepare(q_ref[rows, kc] * scale, k_ref[rows, kc], v_ref[rows, vc],
                                        g[:, kc], None))

    def finish(ci):
        rows = _chunk_rows(ci, chunk)
        for h in range(nh):
            vc = slice(h * dv, (h + 1) * dv)
            o, st_ref[h] = _chunk_finish(*hand.get(h), st_ref[h])
            y_ref[rows, vc] = _head_norm_gate(o, nw_ref[:, vc], gate_ref[rows, vc]).astype(y_ref.dtype)

    _sequential_chunks(tb // chunk, prepare, finish)

    @pl.when(t == pl.num_programs(2) - 1)
    def _():
        for h in range(nh):
            sout_ref[h] = st_ref[h].T


def _gla(proj, lr, row0, nb, t_len, w2, b2, gla_norm, s0, col_q, col_k, col_v, col_g, y_prev):
    _, nh, dk, dv = s0.shape
    rank = lr.shape[1]
    hpb = nh if t_len <= LA_CHUNK else _pick(nh, (2, 1))
    kw, vw = hpb * dk, hpb * dv
    tb = _pick(t_len, (512, 256, 128, 64, 32))
    chunk = min(LA_CHUNK, tb)
    nt = t_len // tb
    rb0 = row0 // tb

    def col(c0, wd):
        return pl.BlockSpec((tb, wd), lambda b, h, t: (rb0 + b * nt + t, c0 // wd + h))

    st = pl.BlockSpec((None, hpb, dk, dv), lambda b, h, t: (b, h, 0, 0))
    kern = functools.partial(_gla_kernel, chunk=chunk, scale=float(dk) ** -0.5)
    args = [proj, proj, proj, proj, lr, w2, b2.reshape(1, nh * dk),
            gla_norm.reshape(1, nh * dv), s0]
    y, s_new = pl.pallas_call(
        kern if y_prev is None else _without_ref(kern, len(args)),
        grid=(nb, nh // hpb, nt),
        in_specs=[col(col_q, kw), col(col_k, kw), col(col_v, vw), col(col_g, vw),
                  pl.BlockSpec((tb, rank), lambda b, h, t: (rb0 + b * nt + t, 0)),
                  pl.BlockSpec((rank, kw), lambda b, h, t: (0, h)),
                  pl.BlockSpec((1, kw), lambda b, h, t: (0, h)),
                  pl.BlockSpec((1, vw), lambda b, h, t: (0, h)),
                  st] + _alias_spec(y_prev),
        out_specs=[pl.BlockSpec((tb, vw), lambda b, h, t: (rb0 + b * nt + t, h)), st],
        out_shape=[jax.ShapeDtypeStruct((proj.shape[0], nh * dv), BF16),
                   jax.ShapeDtypeStruct(s0.shape, F32)],
        scratch_shapes=[pltpu.VMEM((hpb, dv, dk), F32)] + _ChunkScratch.shapes(hpb, chunk, dk, dv),
        input_output_aliases=_alias_map(y_prev, len(args)),
        compiler_params=pltpu.CompilerParams(dimension_semantics=_ARB3),
        name="gla",
    )(*args, *_alias_arg(y_prev))
    return y, s_new


def _lru_kernel(x_ref, gate_ref, cs_ref, h0_ref, cw_ref, cb_ref, wa_ref, ba_ref, wx_ref, bx_ref,
                lam_ref, y_ref, hout_ref, cout_ref, hist_ref, hcar_ref, a_ref, u_ref, h_ref):
    t = pl.program_id(2)
    tb, wd = x_ref.shape
    ncw = cw_ref.shape[0]
    nblk, bd, _ = wa_ref.shape

    @pl.when(t == 0)
    def _():
        hist_ref[...] = jnp.zeros_like(hist_ref)
        hist_ref[SUBLANES - (ncw - 1):SUBLANES, :] = cs_ref[...]
        hcar_ref[...] = jnp.broadcast_to(h0_ref[...], (SUBLANES, wd))

    x = x_ref[...]
    xe = jnp.concatenate([hist_ref[...], x], axis=0)
    cw = cw_ref[...]
    xc = cb_ref[...] + x * cw[ncw - 1:ncw, :]
    for s in range(1, ncw):
        xc = xc + pltpu.roll(xe, s, 0)[SUBLANES:, :] * cw[ncw - 1 - s:ncw - s, :]
    hist_ref[...] = x[tb - SUBLANES:tb, :]

    xb = xc.astype(BF16)
    ra, ri = [], []
    for j in range(nblk):
        xj = xb[:, j * bd:(j + 1) * bd]
        ra.append(_dot(xj, wa_ref[j].astype(BF16)))
        ri.append(_dot(xj, wx_ref[j].astype(BF16)))
    r = _sigmoid(jnp.concatenate(ra, axis=1) + ba_ref[...])
    gi = _sigmoid(jnp.concatenate(ri, axis=1) + bx_ref[...])
    lam = lam_ref[...]
    softplus_neg = jnp.maximum(-lam, 0.0) + jnp.log1p(jnp.exp(-jnp.abs(lam)))
    log_a = -LRU_C * r * softplus_neg
    a = jnp.exp(log_a)
    x2 = 2.0 * log_a
    e2 = jnp.exp(x2)
    small = jnp.where(e2 == 1.0, -x2, (1.0 - e2) * x2 / jnp.log(jnp.where(e2 > 0.25, e2, 0.5)))
    one_m_a2 = jnp.where(x2 < -0.5, 1.0 - e2, small)
    u = jnp.sqrt(jnp.maximum(one_m_a2, 0.0)) * (gi * xc)

    rowi = _tile_row_index(tb, wd)
    s = 1
    while s < SUBLANES:
        ok = rowi >= s
        u = u + jnp.where(ok, a * _tile_roll(u, s), 0.0)
        a = a * jnp.where(ok, _tile_roll(a, s), 1.0)
        s *= 2
    a_ref[...] = a
    u_ref[...] = u

    def body(i, hp):
        rows = pl.ds(pl.multiple_of(i * SUBLANES, SUBLANES), SUBLANES)
        h = u_ref[rows, :] + a_ref[rows, :] * hp
        h_ref[rows, :] = h
        return jnp.broadcast_to(h[SUBLANES - 1:SUBLANES, :], (SUBLANES, wd))

    hp = lax.fori_loop(0, tb // SUBLANES, body, hcar_ref[...], unroll=4)
    hcar_ref[...] = hp
    y_ref[...] = (h_ref[...] * _silu(gate_ref[...])).astype(y_ref.dtype)

    @pl.when(t == pl.num_programs(2) - 1)
    def _():
        hout_ref[...] = hp[0:1, :]
        cout_ref[...] = x[tb - (ncw - 1):tb, :]


def _lru(proj, row0, nb, t_len, conv_state, h0, conv_w, conv_b, wa, ba, wx, bx, lam, col_x, col_g, y_prev):
    nblk, bd, _ = wa.shape
    w = nblk * bd
    ncw = conv_w.shape[0]
    bps = nblk if t_len <= LA_CHUNK else _pick(nblk, (4, 2, 1))
    bw = bps * bd
    tb = _pick(t_len, (512, 256, 128, 64, 32))
    nt = t_len // tb
    rb0 = row0 // tb

    def col(c0):
        return pl.BlockSpec((tb, bw), lambda b, h, t: (rb0 + b * nt + t, c0 // bw + h))

    vec = pl.BlockSpec((1, bw), lambda b, h, t: (0, h))
    blk = pl.BlockSpec((bps, bd, bd), lambda b, h, t: (h, 0, 0))
    hspec = pl.BlockSpec((None, 1, bw), lambda b, h, t: (b, 0, h))
    cspec = pl.BlockSpec((None, ncw - 1, bw), lambda b, h, t: (b, 0, h))
    args = [proj, proj, conv_state, h0.reshape(nb, 1, w), conv_w, conv_b.reshape(1, w),
            wa, ba.reshape(1, w), wx, bx.reshape(1, w), lam.reshape(1, w)]
    y, h_new, c_new = pl.pallas_call(
        _lru_kernel if y_prev is None else _without_ref(_lru_kernel, len(args)),
        grid=(nb, nblk // bps, nt),
        in_specs=[col(col_x), col(col_g), cspec, hspec,
                  pl.BlockSpec((ncw, bw), lambda b, h, t: (0, h)), vec,
                  blk, vec, blk, vec, vec] + _alias_spec(y_prev),
        out_specs=[pl.BlockSpec((tb, bw), lambda b, h, t: (rb0 + b * nt + t, h)), hspec, cspec],
        out_shape=[jax.ShapeDtypeStruct((proj.shape[0], w), BF16),
                   jax.ShapeDtypeStruct((nb, 1, w), F32),
                   jax.ShapeDtypeStruct((nb, ncw - 1, w), F32)],
        scratch_shapes=[pltpu.VMEM((SUBLANES, bw), F32), pltpu.VMEM((SUBLANES, bw), F32),
                        pltpu.VMEM((tb, bw), F32), pltpu.VMEM((tb, bw), F32),
                        pltpu.VMEM((tb, bw), F32)],
        input_output_aliases=_alias_map(y_prev, len(args)),
        compiler_params=pltpu.CompilerParams(dimension_semantics=_ARB3),
        name="rglru",
    )(*args, *_alias_arg(y_prev))
    return y, h_new.reshape(nb, w), c_new


def _merge_kernel(ya_ref, yb_ref, yc_ref, w_ref, m0_ref, m1_ref, m2_ref, o_ref):
    acc = _sigmoid(m0_ref[...]) * _dot(ya_ref[...], w_ref[0].astype(BF16))
    acc = acc + _sigmoid(m1_ref[...]) * _dot(yb_ref[...], w_ref[1].astype(BF16))
    acc = acc + _sigmoid(m2_ref[...]) * _dot(yc_ref[...], w_ref[2].astype(BF16))
    o_ref[...] = acc.astype(o_ref.dtype)


def _merge(ya, yb, yc, wbr, layer, proj, col_m):
    m, bw = ya.shape
    _, nbr, _, d = wbr.shape
    tm = _pick(m, (1408, 1056, 1024, 512, 320, 256, 128, 64, 32, 16))
    tn = next(c for c in (256, 128) if d % c == 0 and col_m % c == 0)
    yspec = pl.BlockSpec((tm, bw), lambda i, j: (i, 0), pipeline_mode=pl.Buffered(1))

    def mg(n):
        return pl.BlockSpec((tm, tn), lambda i, j: (i, (col_m + n * d) // tn + j))

    need = (3 * tm * bw * 2 + 2 * nbr * bw * tn * 4 + nbr * bw * tn * 2
            + 2 * 3 * tm * tn * 4 + 6 * tm * tn * 4)
    return pl.pallas_call(
        _merge_kernel,
        grid=(m // tm, d // tn),
        in_specs=[yspec, yspec, yspec,
                  pl.BlockSpec((None, nbr, bw, tn), lambda i, j: (layer, 0, 0, j)),
                  mg(0), mg(1), mg(2)],
        out_specs=pl.BlockSpec((tm, tn), lambda i, j: (i, j)),
        out_shape=jax.ShapeDtypeStruct((m, d), BF16),
        compiler_params=pltpu.CompilerParams(
            dimension_semantics=_ARB2, vmem_limit_bytes=_vmem_limit(need)),
        name="merge",
    )(ya, yb, yc, wbr, proj, proj, proj)


def kernel(x_prompt, x_sample, state_hgrn, state_lru_h, state_lru_conv, state_gla, norm_pre, norm_post, w_in, hg_lb_logits, hg_norm, lru_conv_w, lru_conv_b, lru_wa, lru_ba, lru_wx, lru_bx, lru_lambda, gla_w2, gla_b2, gla_norm, w_branch, w_out):
    bp, tp, d = x_prompt.shape
    bs, ts, _ = x_sample.shape
    depth, _, in_cols = w_in.shape
    _, _, hg_h, hg_dk, hg_dv = state_hgrn.shape
    hg_w = hg_h * hg_dk
    lru_w = state_lru_h.shape[-1]
    _, _, gl_h, gl_dk, gl_dv = state_gla.shape
    gl_kw, gl_vw = gl_h * gl_dk, gl_h * gl_dv
    rank = gla_w2.shape[1]
    dt = x_prompt.dtype

    c_hg = 0
    c_lx = 4 * hg_w
    c_lg = c_lx + lru_w
    c_q = c_lg + lru_w
    c_k = c_q + gl_kw
    c_v = c_k + gl_kw
    c_lr = c_v + gl_vw
    c_cg = c_lr
    c_m = c_cg + gl_vw
    n_main = in_cols - rank
    assert c_m + 3 * d == n_main

    lb_sm = jax.nn.softmax(hg_lb_logits.astype(F32), axis=0)
    lb_all = jnp.cumsum(lb_sm, axis=0) - lb_sm[0:1]
    w_in_t = jnp.transpose(w_in, (0, 2, 1))

    mp, ms = bp * tp, bs * ts
    xs = (x_prompt.reshape(mp, d), x_sample.reshape(ms, d))

    zeros = lambda shape: jnp.zeros(shape, dt)
    outs = {k: [] for k in ("hg_p", "hg_s", "lh_p", "lh_s", "lc_p", "lc_s", "gl_p", "gl_s")}
    z = _prenorm(xs, norm_pre[0])
    for l in range(depth):
        proj = _matmul_t(z, w_in_t, l, n_main, c_lr, rank, "in_proj")
        lr = _matmul_t(z, w_in_t, l, rank, 0, c_lr, "lr_proj")

        groups = (
            ("p", 0, bp, tp, zeros((bp,) + state_hgrn.shape[2:]), zeros((bp, lru_w)),
             zeros((bp,) + state_lru_conv.shape[2:]), zeros((bp,) + state_gla.shape[2:])),
            ("s", mp, bs, ts, state_hgrn[l], state_lru_h[l], state_lru_conv[l], state_gla[l]),
        )
        ya = jnp.zeros((mp + ms, hg_w), BF16)
        yb = jnp.zeros((mp + ms, lru_w), BF16)
        yc = jnp.zeros((mp + ms, gl_vw), BF16)
        for tag, row0, nb, t_len, s_hg, s_lh, s_lc, s_gl in groups:
            ya, s = _hgrn(proj, row0, nb, t_len, lb_all[l], hg_norm[l], s_hg, c_hg, ya)
            outs["hg_" + tag].append(s)
            yb, hn, cn = _lru(proj, row0, nb, t_len, s_lc, s_lh, lru_conv_w[l], lru_conv_b[l],
                              lru_wa[l], lru_ba[l], lru_wx[l], lru_bx[l], lru_lambda[l], c_lx, c_lg, yb)
            outs["lh_" + tag].append(hn); outs["lc_" + tag].append(cn)
            yc, s = _gla(proj, lr, row0, nb, t_len, gla_w2[l], gla_b2[l], gla_norm[l], s_gl,
                         c_q, c_k, c_v, c_cg, yc)
            outs["gl_" + tag].append(s)

        merged = _merge(ya, yb, yc, w_branch, l, proj, c_m)
        out = _matmul(merged, w_out, l, "out_proj")
        if l < depth - 1:
            x, z = _postnorm(xs, out, norm_post[l], mp, ms, norm_pre[l + 1])
            xs = (x,)
        else:
            xs = _postnorm(xs, out, norm_post[l], mp, ms, None)

    yp = xs[0].reshape(bp, tp, d)
    ys = xs[1].reshape(bs, ts, d)
    st = lambda k: jnp.stack(outs[k])
    return (yp, ys, st("hg_p"), st("hg_s"), st("lh_p"), st("lh_s"),
            st("lc_p"), st("lc_s"), st("gl_p"), st("gl_s"))
```

```python
import functools
import math

import jax
import jax.numpy as jnp
from jax import lax
from jax.experimental import pallas as pl
from jax.experimental.pallas import tpu as pltpu

F32 = jnp.float32
BF16 = jnp.bfloat16

EPS = 1e-6
F_FLOOR = 1e-6
LRU_C = 8.0
GLA_TAU = 16.0
SUBLANES = 8
LANES = 128
LA_CHUNK = 64
V7X_VMEM_LIMIT = 60000 * 1024


def _vmem_limit(nbytes):
    return int(min(V7X_VMEM_LIMIT, max(32 * 1024 * 1024, nbytes * 5 // 4 + (4 << 20))))


def _pick(n, candidates):
    for c in candidates:
        if n % c == 0:
            return c
    return n


def _dot(a, b):
    return jnp.dot(a, b, preferred_element_type=F32)


def _dot_nt(a, b):
    return lax.dot_general(a, b, (((1,), (1,)), ((), ())), preferred_element_type=F32)


def _dot_tn(a, b):
    return lax.dot_general(a, b, (((0,), (0,)), ((), ())), preferred_element_type=F32)


def _sigmoid(x):
    return jax.nn.sigmoid(x)


def _silu(x):
    return x * jax.nn.sigmoid(x)


def _without_ref(kernel, idx):
    def wrapped(*refs):
        return kernel(*refs[:idx], *refs[idx + 1:])
    return wrapped


def _alias_spec(y_prev):
    return [] if y_prev is None else [pl.BlockSpec(memory_space=pl.ANY)]


def _alias_arg(y_prev):
    return [] if y_prev is None else [y_prev]


def _alias_map(y_prev, idx):
    return {} if y_prev is None else {idx: 0}


_ARB2 = ("arbitrary", "arbitrary")
_ARB3 = ("arbitrary", "arbitrary", "arbitrary")


def _rms(x, w):
    return x * lax.rsqrt(jnp.mean(x * x, axis=-1, keepdims=True) + EPS) * w


def _group_specs(mp, ms, d):
    tr = next(c for c in (256, 128, 64, 32, 16, 8) if mp % c == 0 and ms % c == 0)
    nbp = mp // tr
    pspec = pl.BlockSpec((tr, d), lambda i: (jnp.minimum(i, nbp - 1), 0))
    sspec = pl.BlockSpec((tr, d), lambda i: (jnp.maximum(i - nbp, 0), 0))
    return tr, nbp, pspec, sspec


def _prenorm2_kernel(xp_ref, xs_ref, w_ref, z_ref, *, nbp):
    x = jnp.where(pl.program_id(0) < nbp, xp_ref[...], xs_ref[...])
    z_ref[...] = _rms(x, w_ref[...]).astype(z_ref.dtype)


def _prenorm(xs, w):
    d = w.shape[0]
    mp, ms = xs[0].shape[0], xs[1].shape[0]
    tr, nbp, pspec, sspec = _group_specs(mp, ms, d)
    return pl.pallas_call(
        functools.partial(_prenorm2_kernel, nbp=nbp),
        grid=((mp + ms) // tr,),
        in_specs=[pspec, sspec, pl.BlockSpec((1, d), lambda i: (0, 0))],
        out_specs=pl.BlockSpec((tr, d), lambda i: (i, 0)),
        out_shape=jax.ShapeDtypeStruct((mp + ms, d), BF16),
        compiler_params=pltpu.CompilerParams(
            dimension_semantics=("arbitrary",), vmem_limit_bytes=_vmem_limit(2 * 3 * tr * d * 4)),
        name="prenorm",
    )(xs[0], xs[1], w.reshape(1, d))


def _postnorm_kernel(*refs, nbp, n_in, last):
    i = pl.program_id(0)
    o_ref, w_ref = refs[n_in], refs[n_in + 1]
    x = refs[0][...] if n_in == 1 else jnp.where(i < nbp, refs[0][...], refs[1][...])
    y = x + _rms(o_ref[...], w_ref[...])
    if last:
        yp_ref, ys_ref = refs[n_in + 2:]

        @pl.when(i < nbp)
        def _():
            yp_ref[...] = y

        @pl.when(i >= nbp)
        def _():
            ys_ref[...] = y
    else:
        wn_ref, y_ref, z_ref = refs[n_in + 2:]
        y_ref[...] = y
        z_ref[...] = _rms(y, wn_ref[...]).astype(z_ref.dtype)


def _postnorm(xs, out, w, mp, ms, w_next):
    d = w.shape[0]
    tr, nbp, pspec, sspec = _group_specs(mp, ms, d)
    full = pl.BlockSpec((tr, d), lambda i: (i, 0))
    vec = pl.BlockSpec((1, d), lambda i: (0, 0))
    in_specs = ([pspec, sspec] if len(xs) == 2 else [full]) + [full, vec]
    args = [*xs, out, w.reshape(1, d)]
    if w_next is None:
        out_specs = [pspec, sspec]
        out_shape = [jax.ShapeDtypeStruct((mp, d), F32), jax.ShapeDtypeStruct((ms, d), F32)]
    else:
        in_specs.append(vec)
        args.append(w_next.reshape(1, d))
        out_specs = [full, full]
        out_shape = [jax.ShapeDtypeStruct((mp + ms, d), F32), jax.ShapeDtypeStruct((mp + ms, d), BF16)]
    return pl.pallas_call(
        functools.partial(_postnorm_kernel, nbp=nbp, n_in=len(xs), last=w_next is None),
        grid=((mp + ms) // tr,),
        in_specs=in_specs, out_specs=out_specs, out_shape=out_shape,
        compiler_params=pltpu.CompilerParams(
            dimension_semantics=("arbitrary",),
            vmem_limit_bytes=_vmem_limit(2 * (len(in_specs) + len(out_specs)) * tr * d * 4)),
        name="postnorm",
    )(*args)


def _mm_kernel(a_ref, w_ref, o_ref):
    o_ref[...] = _dot(a_ref[...], w_ref[...].astype(BF16))


def _mm_t_kernel(a_ref, wt_ref, o_ref):
    o_ref[...] = _dot_nt(a_ref[...], wt_ref[0].astype(BF16))


BIG_TM = (2112, 1408, 1056, 1024, 512, 320, 256, 128, 64, 32, 16)


def _matmul(a, w, layer, name):
    m, k = a.shape
    n = w.shape[2]
    tm = _pick(m, BIG_TM)
    tn = _pick(n, (512, 256, 128))
    need = tm * k * 2 + 2 * k * tn * 4 + k * tn * 2 + 3 * tm * tn * 4
    return pl.pallas_call(
        _mm_kernel,
        grid=(m // tm, n // tn),
        in_specs=[pl.BlockSpec((tm, k), lambda i, j: (i, 0), pipeline_mode=pl.Buffered(1)),
                  pl.BlockSpec((None, k, tn), lambda i, j: (layer, 0, j))],
        out_specs=pl.BlockSpec((tm, tn), lambda i, j: (i, j)),
        out_shape=jax.ShapeDtypeStruct((m, n), F32),
        compiler_params=pltpu.CompilerParams(
            dimension_semantics=_ARB2, vmem_limit_bytes=_vmem_limit(need)),
        name=name,
    )(a, w)


def _matmul_t(a, wt, layer, n_out, skip_start, skip, name):
    m, k = a.shape
    tm = _pick(m, BIG_TM)
    tn = next(c for c in (512, 256, 128, 64, 32, 16, 8)
              if n_out % c == 0 and skip_start % c == 0 and (c <= n_out // 2 or c == n_out))
    assert skip % SUBLANES == 0

    def w_index(i, j):
        row = j * tn + jnp.where(j * tn >= skip_start, skip, 0)
        return layer, pl.multiple_of(row, SUBLANES), 0

    need = tm * k * 2 + 2 * k * tn * 4 + k * tn * 2 + 3 * tm * tn * 4
    return pl.pallas_call(
        _mm_t_kernel,
        grid=(m // tm, n_out // tn),
        in_specs=[pl.BlockSpec((tm, k), lambda i, j: (i, 0), pipeline_mode=pl.Buffered(1)),
                  pl.BlockSpec((pl.Element(1), pl.Element(tn), pl.Element(k)), w_index)],
        out_specs=pl.BlockSpec((tm, tn), lambda i, j: (i, j)),
        out_shape=jax.ShapeDtypeStruct((m, n_out), F32),
        compiler_params=pltpu.CompilerParams(
            dimension_semantics=_ARB2, vmem_limit_bytes=_vmem_limit(need)),
        name=name,
    )(a, wt)


def _tile_row_index(n, d):
    return lax.broadcasted_iota(jnp.int32, (n, d), 0) & (SUBLANES - 1)


def _tile_roll(x, s):
    n, d = x.shape
    return pltpu.roll(x.reshape(n // SUBLANES, SUBLANES, d), s, 1).reshape(n, d)


def _tile_cumsum(g, rowi):
    c = g
    s = 1
    while s < SUBLANES:
        c = c + jnp.where(rowi >= s, _tile_roll(c, s), 0.0)
        s *= 2
    return c


def _tile_row_bcast(x, r):
    n, d = x.shape
    return jnp.concatenate(
        [jnp.broadcast_to(x[SUBLANES * i + r:SUBLANES * i + r + 1, :], (SUBLANES, d))
         for i in range(n // SUBLANES)], axis=0)


def _chunk_prepare(q, k, v, g, g_min):
    L, dk = q.shape
    nt = L // SUBLANES
    rowk = _tile_row_index(L, dk)

    c = _tile_cumsum(g, rowk)
    tot = [c[SUBLANES * i + SUBLANES - 1:SUBLANES * (i + 1), :] for i in range(nt)]
    qt = q * jnp.exp(c)
    knew = k * jnp.exp(_tile_row_bcast(c, SUBLANES - 1) - c)

    row = lax.broadcasted_iota(jnp.int32, (L, L), 0)
    col = lax.broadcasted_iota(jnp.int32, (L, L), 1)
    offs = jnp.where((row >> 3) == (col >> 3), row - col, -1)
    half = SUBLANES // 2
    if g_min is not None and -g_min * half < 80.0:
        cm = c - _tile_row_bcast(c, half - 1)
        inner = _dot_nt((q * jnp.exp(cm)).astype(BF16), (k * jnp.exp(-cm)).astype(BF16))
        band = jnp.where(offs >= 0, inner, 0.0)
    else:
        eg = jnp.exp(g)
        w = k
        band = jnp.where(offs == 0, jnp.sum(q * w, axis=1, keepdims=True), 0.0)
        for d in range(1, SUBLANES):
            w = _tile_roll(w, 1) * eg
            band = jnp.where(offs == d, jnp.sum(q * w, axis=1, keepdims=True), band)

    tiles = []
    arows = [jnp.zeros((SUBLANES, L), F32)]
    qhat = [qt[0:SUBLANES]]
    er = None
    for i in range(nt):
        lo, hi = SUBLANES * i, SUBLANES * (i + 1)
        if i > 0:
            pad = jnp.zeros((L - lo, dk), F32)
            kh = jnp.concatenate(tiles + [pad], axis=0).astype(BF16)
            arows.append(_dot_nt(qt[lo:hi].astype(BF16), kh))
            qhat.append(qt[lo:hi] * er)
        dcy = jnp.exp(tot[i])
        tiles = [t * dcy for t in tiles] + [knew[lo:hi]]
        er = dcy if er is None else er * dcy

    khat = jnp.concatenate(tiles, axis=0).astype(BF16)
    qh = jnp.concatenate(qhat, axis=0).astype(BF16)
    scores = (jnp.concatenate(arows, axis=0) + band).astype(BF16)
    return scores, v.astype(BF16), qh, khat, er


def _chunk_finish(scores, vb, qh, khat, er, st):
    o = _dot(scores, vb) + _dot_nt(qh, st.astype(BF16))
    return o, st * er + _dot_tn(vb, khat)


class _ChunkScratch:
    def __init__(self, refs):
        self.sc, self.vb, self.qh, self.kh, self.er = refs

    @staticmethod
    def shapes(nh, chunk, dk, dv):
        return [pltpu.VMEM((nh, chunk, chunk), BF16), pltpu.VMEM((nh, chunk, dv), BF16),
                pltpu.VMEM((nh, chunk, dk), BF16), pltpu.VMEM((nh, chunk, dk), BF16),
                pltpu.VMEM((nh, SUBLANES, dk), F32)]

    def put(self, h, scores, vb, qh, khat, er):
        self.sc[h] = scores
        self.vb[h] = vb
        self.qh[h] = qh
        self.kh[h] = khat
        self.er[h] = jnp.broadcast_to(er, self.er.shape[1:])

    def get(self, h):
        return self.sc[h], self.vb[h], self.qh[h], self.kh[h], self.er[h][0:1, :]


def _pipelined_chunks(n, prepare, finish):
    prepare(0)

    def body(ci, carry):
        finish(ci - 1)
        prepare(ci)
        return carry

    lax.fori_loop(1, n, body, 0)
    finish(n - 1)


def _sequential_chunks(n, prepare, finish):
    def body(ci, carry):
        prepare(ci)
        finish(ci)
        return carry

    lax.fori_loop(0, n, body, 0)


def _head_norm_gate(o, w, gate):
    on = o * lax.rsqrt(jnp.mean(o * o, axis=-1, keepdims=True) + EPS)
    return on * w * _silu(gate)


def _chunk_rows(ci, chunk):
    start = ci * chunk
    return pl.ds(start if isinstance(start, int) else pl.multiple_of(start, chunk), chunk)


def _hgrn_kernel(q_ref, f_ref, i_ref, gate_ref, lb_ref, nw_ref, s0_ref,
                 y_ref, sout_ref, st_ref, *scratch, chunk):
    t = pl.program_id(2)
    tb = q_ref.shape[0]
    nh, dk, dv = s0_ref.shape
    hand = _ChunkScratch(scratch)
    g_min = math.log(F_FLOOR) - 1e-3

    @pl.when(t == 0)
    def _():
        for h in range(nh):
            st_ref[h] = s0_ref[h].T

    def prepare(ci):
        rows = _chunk_rows(ci, chunk)
        for h in range(nh):
            cols = slice(h * dk, (h + 1) * dk)
            lb = lb_ref[:, cols]
            f = lb + (1.0 - lb) * _sigmoid(f_ref[rows, cols])
            g = jnp.log(jnp.maximum(f, F_FLOOR))
            hand.put(h, *_chunk_prepare(_silu(q_ref[rows, cols]), 1.0 - f, i_ref[rows, cols], g, g_min))

    def finish(ci):
        rows = _chunk_rows(ci, chunk)
        for h in range(nh):
            cols = slice(h * dk, (h + 1) * dk)
            o, st_ref[h] = _chunk_finish(*hand.get(h), st_ref[h])
            y = _head_norm_gate(o, nw_ref[:, cols], gate_ref[rows, cols])
            y_ref[rows, cols] = y.astype(y_ref.dtype)

    _pipelined_chunks(tb // chunk, prepare, finish)

    @pl.when(t == pl.num_programs(2) - 1)
    def _():
        for h in range(nh):
            sout_ref[h] = st_ref[h].T


def _hgrn(proj, row0, nb, t_len, lb, hg_norm, s0, col0, y_prev):
    _, nh, dk, dv = s0.shape
    w = nh * dk
    hpb = nh if t_len <= LA_CHUNK else _pick(nh, (8, 4, 2, 1))
    bw = hpb * dk
    tb = _pick(t_len, (1024, 512, 256, 128, 64, 32))
    chunk = min(LA_CHUNK, tb)
    nt = t_len // tb
    rb0 = row0 // tb

    def col(g):
        return pl.BlockSpec((tb, bw), lambda b, h, t: (rb0 + b * nt + t, (col0 + g * w) // bw + h))

    vec = pl.BlockSpec((1, bw), lambda b, h, t: (0, h))
    st = pl.BlockSpec((None, hpb, dk, dv), lambda b, h, t: (b, h, 0, 0))
    kern = functools.partial(_hgrn_kernel, chunk=chunk)
    args = [proj, proj, proj, proj, lb.reshape(1, w), hg_norm.reshape(1, w), s0]
    y, s_new = pl.pallas_call(
        kern if y_prev is None else _without_ref(kern, len(args)),
        grid=(nb, nh // hpb, nt),
        in_specs=[col(0), col(1), col(2), col(3), vec, vec, st] + _alias_spec(y_prev),
        out_specs=[pl.BlockSpec((tb, bw), lambda b, h, t: (rb0 + b * nt + t, h)), st],
        out_shape=[jax.ShapeDtypeStruct((proj.shape[0], w), BF16),
                   jax.ShapeDtypeStruct(s0.shape, F32)],
        scratch_shapes=[pltpu.VMEM((hpb, dv, dk), F32)] + _ChunkScratch.shapes(hpb, chunk, dk, dv),
        input_output_aliases=_alias_map(y_prev, len(args)),
        compiler_params=pltpu.CompilerParams(
            dimension_semantics=_ARB3,
            vmem_limit_bytes=_vmem_limit(2 * tb * bw * (4 * 4 + 2) + 5 * hpb * dk * dv * 4)),
        name="hgrn2",
    )(*args, *_alias_arg(y_prev))
    return y, s_new


def _gla_kernel(q_ref, k_ref, v_ref, gate_ref, lr_ref, w2_ref, b2_ref, nw_ref, s0_ref,
                y_ref, sout_ref, st_ref, *scratch, chunk, scale):
    t = pl.program_id(2)
    tb = q_ref.shape[0]
    nh, dk, dv = s0_ref.shape
    hand = _ChunkScratch(scratch)

    @pl.when(t == 0)
    def _():
        for h in range(nh):
            st_ref[h] = s0_ref[h].T

    def prepare(ci):
        rows = _chunk_rows(ci, chunk)
        logits = _dot(lr_ref[rows, :].astype(BF16), w2_ref[...].astype(BF16)) + b2_ref[...]
        g = (jnp.minimum(logits, 0.0) - jnp.log1p(jnp.exp(-jnp.abs(logits)))) / GLA_TAU
        for h in range(nh):
            kc, vc = slice(h * dk, (h + 1) * dk), slice(h * dv, (h + 1) * dv)
            hand.put(h, *_chunk_prepare(q_ref[rows, kc] * scale, k_ref[rows, kc], v_ref[rows, vc],
                                        g[:, kc], None))

    def finish(ci):
        rows = _chunk_rows(ci, chunk)
        for h in range(nh):
            vc = slice(h * dv, (h + 1) * dv)
            o, st_ref[h] = _chunk_finish(*hand.get(h), st_ref[h])
            y_ref[rows, vc] = _head_norm_gate(o, nw_ref[:, vc], gate_ref[rows, vc]).astype(y_ref.dtype)

    _sequential_chunks(tb // chunk, prepare, finish)

    @pl.when(t == pl.num_programs(2) - 1)
    def _():
        for h in range(nh):
            sout_ref[h] = st_ref[h].T


def _gla(proj, lr, row0, nb, t_len, w2, b2, gla_norm, s0, col_q, col_k, col_v, col_g, y_prev):
    _, nh, dk, dv = s0.shape
    rank = lr.shape[1]
    hpb = nh if t_len <= LA_CHUNK else _pick(nh, (4, 2, 1))
    kw, vw = hpb * dk, hpb * dv
    tb = _pick(t_len, (512, 256, 128, 64, 32))
    chunk = min(LA_CHUNK, tb)
    nt = t_len // tb
    rb0 = row0 // tb

    def col(c0, wd):
        return pl.BlockSpec((tb, wd), lambda b, h, t: (rb0 + b * nt + t, c0 // wd + h))

    st = pl.BlockSpec((None, hpb, dk, dv), lambda b, h, t: (b, h, 0, 0))
    kern = functools.partial(_gla_kernel, chunk=chunk, scale=float(dk) ** -0.5)
    args = [proj, proj, proj, proj, lr, w2, b2.reshape(1, nh * dk),
            gla_norm.reshape(1, nh * dv), s0]
    y, s_new = pl.pallas_call(
        kern if y_prev is None else _without_ref(kern, len(args)),
        grid=(nb, nh // hpb, nt),
        in_specs=[col(col_q, kw), col(col_k, kw), col(col_v, vw), col(col_g, vw),
                  pl.BlockSpec((tb, rank), lambda b, h, t: (rb0 + b * nt + t, 0)),
                  pl.BlockSpec((rank, kw), lambda b, h, t: (0, h)),
                  pl.BlockSpec((1, kw), lambda b, h, t: (0, h)),
                  pl.BlockSpec((1, vw), lambda b, h, t: (0, h)),
                  st] + _alias_spec(y_prev),
        out_specs=[pl.BlockSpec((tb, vw), lambda b, h, t: (rb0 + b * nt + t, h)), st],
        out_shape=[jax.ShapeDtypeStruct((proj.shape[0], nh * dv), BF16),
                   jax.ShapeDtypeStruct(s0.shape, F32)],
        scratch_shapes=[pltpu.VMEM((hpb, dv, dk), F32)] + _ChunkScratch.shapes(hpb, chunk, dk, dv),
        input_output_aliases=_alias_map(y_prev, len(args)),
        compiler_params=pltpu.CompilerParams(
            dimension_semantics=_ARB3,
            vmem_limit_bytes=_vmem_limit(2 * tb * (2 * kw * 4 + 2 * vw * 4 + vw * 2) + 5 * hpb * dk * dv * 4)),
        name="gla",
    )(*args, *_alias_arg(y_prev))
    return y, s_new


def _lru_kernel(x_ref, gate_ref, cs_ref, h0_ref, cw_ref, cb_ref, wa_ref, ba_ref, wx_ref, bx_ref,
                lam_ref, y_ref, hout_ref, cout_ref, hist_ref, hcar_ref, a_ref, u_ref, h_ref):
    t = pl.program_id(2)
    tb, wd = x_ref.shape
    ncw = cw_ref.shape[0]
    nblk, bd, _ = wa_ref.shape

    @pl.when(t == 0)
    def _():
        hist_ref[...] = jnp.zeros_like(hist_ref)
        hist_ref[SUBLANES - (ncw - 1):SUBLANES, :] = cs_ref[...]
        hcar_ref[...] = jnp.broadcast_to(h0_ref[...], (SUBLANES, wd))

    x = x_ref[...]
    xe = jnp.concatenate([hist_ref[...], x], axis=0)
    cw = cw_ref[...]
    xc = cb_ref[...] + x * cw[ncw - 1:ncw, :]
    for s in range(1, ncw):
        xc = xc + pltpu.roll(xe, s, 0)[SUBLANES:, :] * cw[ncw - 1 - s:ncw - s, :]
    hist_ref[...] = x[tb - SUBLANES:tb, :]

    xb = xc.astype(BF16)
    ra, ri = [], []
    for j in range(nblk):
        xj = xb[:, j * bd:(j + 1) * bd]
        ra.append(_dot(xj, wa_ref[j].astype(BF16)))
        ri.append(_dot(xj, wx_ref[j].astype(BF16)))
    r = _sigmoid(jnp.concatenate(ra, axis=1) + ba_ref[...])
    gi = _sigmoid(jnp.concatenate(ri, axis=1) + bx_ref[...])
    lam = lam_ref[...]
    softplus_neg = jnp.maximum(-lam, 0.0) + jnp.log1p(jnp.exp(-jnp.abs(lam)))
    log_a = -LRU_C * r * softplus_neg
    a = jnp.exp(log_a)
    x2 = 2.0 * log_a
    e2 = jnp.exp(x2)
    small = jnp.where(e2 == 1.0, -x2, (1.0 - e2) * x2 / jnp.log(jnp.where(e2 > 0.25, e2, 0.5)))
    one_m_a2 = jnp.where(x2 < -0.5, 1.0 - e2, small)
    u = jnp.sqrt(jnp.maximum(one_m_a2, 0.0)) * (gi * xc)

    rowi = _tile_row_index(tb, wd)
    s = 1
    while s < SUBLANES:
        ok = rowi >= s
        u = u + jnp.where(ok, a * _tile_roll(u, s), 0.0)
        a = a * jnp.where(ok, _tile_roll(a, s), 1.0)
        s *= 2
    a_ref[...] = a
    u_ref[...] = u

    def body(i, hp):
        rows = pl.ds(pl.multiple_of(i * SUBLANES, SUBLANES), SUBLANES)
        h = u_ref[rows, :] + a_ref[rows, :] * hp
        h_ref[rows, :] = h
        return jnp.broadcast_to(h[SUBLANES - 1:SUBLANES, :], (SUBLANES, wd))

    hp = lax.fori_loop(0, tb // SUBLANES, body, hcar_ref[...], unroll=4)
    hcar_ref[...] = hp
    y_ref[...] = (h_ref[...] * _silu(gate_ref[...])).astype(y_ref.dtype)

    @pl.when(t == pl.num_programs(2) - 1)
    def _():
        hout_ref[...] = hp[0:1, :]
        cout_ref[...] = x[tb - (ncw - 1):tb, :]


def _lru(proj, row0, nb, t_len, conv_state, h0, conv_w, conv_b, wa, ba, wx, bx, lam, col_x, col_g, y_prev):
    nblk, bd, _ = wa.shape
    w = nblk * bd
    ncw = conv_w.shape[0]
    bps = nblk if t_len <= LA_CHUNK else _pick(nblk, (4, 2, 1))
    bw = bps * bd
    tb = _pick(t_len, (512, 256, 128, 64, 32))
    nt = t_len // tb
    rb0 = row0 // tb

    def col(c0):
        return pl.BlockSpec((tb, bw), lambda b, h, t: (rb0 + b * nt + t, c0 // bw + h))

    vec = pl.BlockSpec((1, bw), lambda b, h, t: (0, h))
    blk = pl.BlockSpec((bps, bd, bd), lambda b, h, t: (h, 0, 0))
    hspec = pl.BlockSpec((None, 1, bw), lambda b, h, t: (b, 0, h))
    cspec = pl.BlockSpec((None, ncw - 1, bw), lambda b, h, t: (b, 0, h))
    args = [proj, proj, conv_state, h0.reshape(nb, 1, w), conv_w, conv_b.reshape(1, w),
            wa, ba.reshape(1, w), wx, bx.reshape(1, w), lam.reshape(1, w)]
    y, h_new, c_new = pl.pallas_call(
        _lru_kernel if y_prev is None else _without_ref(_lru_kernel, len(args)),
        grid=(nb, nblk // bps, nt),
        in_specs=[col(col_x), col(col_g), cspec, hspec,
                  pl.BlockSpec((ncw, bw), lambda b, h, t: (0, h)), vec,
                  blk, vec, blk, vec, vec] + _alias_spec(y_prev),
        out_specs=[pl.BlockSpec((tb, bw), lambda b, h, t: (rb0 + b * nt + t, h)), hspec, cspec],
        out_shape=[jax.ShapeDtypeStruct((proj.shape[0], w), BF16),
                   jax.ShapeDtypeStruct((nb, 1, w), F32),
                   jax.ShapeDtypeStruct((nb, ncw - 1, w), F32)],
        scratch_shapes=[pltpu.VMEM((SUBLANES, bw), F32), pltpu.VMEM((SUBLANES, bw), F32),
                        pltpu.VMEM((tb, bw), F32), pltpu.VMEM((tb, bw), F32),
                        pltpu.VMEM((tb, bw), F32)],
        input_output_aliases=_alias_map(y_prev, len(args)),
        compiler_params=pltpu.CompilerParams(dimension_semantics=_ARB3),
        name="rglru",
    )(*args, *_alias_arg(y_prev))
    return y, h_new.reshape(nb, w), c_new


def _merge_kernel(ya_ref, yb_ref, yc_ref, w_ref, m0_ref, m1_ref, m2_ref, o_ref):
    acc = _sigmoid(m0_ref[...]) * _dot(ya_ref[...], w_ref[0].astype(BF16))
    acc = acc + _sigmoid(m1_ref[...]) * _dot(yb_ref[...], w_ref[1].astype(BF16))
    acc = acc + _sigmoid(m2_ref[...]) * _dot(yc_ref[...], w_ref[2].astype(BF16))
    o_ref[...] = acc.astype(o_ref.dtype)


def _merge(ya, yb, yc, wbr, layer, proj, col_m):
    m, bw = ya.shape
    _, nbr, _, d = wbr.shape
    tm = _pick(m, (1408, 1056, 1024, 512, 320, 256, 128, 64, 32, 16))
    tn = next(c for c in (256, 128) if d % c == 0 and col_m % c == 0)
    yspec = pl.BlockSpec((tm, bw), lambda i, j: (i, 0), pipeline_mode=pl.Buffered(1))

    def mg(n):
        return pl.BlockSpec((tm, tn), lambda i, j: (i, (col_m + n * d) // tn + j))

    need = (3 * tm * bw * 2 + 2 * nbr * bw * tn * 4 + nbr * bw * tn * 2
            + 2 * 3 * tm * tn * 4 + 6 * tm * tn * 4)
    return pl.pallas_call(
        _merge_kernel,
        grid=(m // tm, d // tn),
        in_specs=[yspec, yspec, yspec,
                  pl.BlockSpec((None, nbr, bw, tn), lambda i, j: (layer, 0, 0, j)),
                  mg(0), mg(1), mg(2)],
        out_specs=pl.BlockSpec((tm, tn), lambda i, j: (i, j)),
        out_shape=jax.ShapeDtypeStruct((m, d), BF16),
        compiler_params=pltpu.CompilerParams(
            dimension_semantics=_ARB2, vmem_limit_bytes=_vmem_limit(need)),
        name="merge",
    )(ya, yb, yc, wbr, proj, proj, proj)


def kernel(x_prompt, x_sample, state_hgrn, state_lru_h, state_lru_conv, state_gla, norm_pre, norm_post, w_in, hg_lb_logits, hg_norm, lru_conv_w, lru_conv_b, lru_wa, lru_ba, lru_wx, lru_bx, lru_lambda, gla_w2, gla_b2, gla_norm, w_branch, w_out):
    bp, tp, d = x_prompt.shape
    bs, ts, _ = x_sample.shape
    depth, _, in_cols = w_in.shape
    _, _, hg_h, hg_dk, hg_dv = state_hgrn.shape
    hg_w = hg_h * hg_dk
    lru_w = state_lru_h.shape[-1]
    _, _, gl_h, gl_dk, gl_dv = state_gla.shape
    gl_kw, gl_vw = gl_h * gl_dk, gl_h * gl_dv
    rank = gla_w2.shape[1]
    dt = x_prompt.dtype

    c_hg = 0
    c_lx = 4 * hg_w
    c_lg = c_lx + lru_w
    c_q = c_lg + lru_w
    c_k = c_q + gl_kw
    c_v = c_k + gl_kw
    c_lr = c_v + gl_vw
    c_cg = c_lr
    c_m = c_cg + gl_vw
    n_main = in_cols - rank
    assert c_m + 3 * d == n_main

    lb_sm = jax.nn.softmax(hg_lb_logits.astype(F32), axis=0)
    lb_all = jnp.cumsum(lb_sm, axis=0) - lb_sm[0:1]
    w_in_t = jnp.transpose(w_in, (0, 2, 1))

    mp, ms = bp * tp, bs * ts
    xs = (x_prompt.reshape(mp, d), x_sample.reshape(ms, d))

    zeros = lambda shape: jnp.zeros(shape, dt)
    outs = {k: [] for k in ("hg_p", "hg_s", "lh_p", "lh_s", "lc_p", "lc_s", "gl_p", "gl_s")}
    z = _prenorm(xs, norm_pre[0])
    for l in range(depth):
        proj = _matmul_t(z, w_in_t, l, n_main, c_lr, rank, "in_proj")
        lr = _matmul_t(z, w_in_t, l, rank, 0, c_lr, "lr_proj")

        groups = (
            ("p", 0, bp, tp, zeros((bp,) + state_hgrn.shape[2:]), zeros((bp, lru_w)),
             zeros((bp,) + state_lru_conv.shape[2:]), zeros((bp,) + state_gla.shape[2:])),
            ("s", mp, bs, ts, state_hgrn[l], state_lru_h[l], state_lru_conv[l], state_gla[l]),
        )
        ya = jnp.zeros((mp + ms, hg_w), BF16)
        yb = jnp.zeros((mp + ms, lru_w), BF16)
        yc = jnp.zeros((mp + ms, gl_vw), BF16)
        for tag, row0, nb, t_len, s_hg, s_lh, s_lc, s_gl in groups:
            ya, s = _hgrn(proj, row0, nb, t_len, lb_all[l], hg_norm[l], s_hg, c_hg, ya)
            outs["hg_" + tag].append(s)
            yb, hn, cn = _lru(proj, row0, nb, t_len, s_lc, s_lh, lru_conv_w[l], lru_conv_b[l],
                              lru_wa[l], lru_ba[l], lru_wx[l], lru_bx[l], lru_lambda[l], c_lx, c_lg, yb)
            outs["lh_" + tag].append(hn); outs["lc_" + tag].append(cn)
            yc, s = _gla(proj, lr, row0, nb, t_len, gla_w2[l], gla_b2[l], gla_norm[l], s_gl,
                         c_q, c_k, c_v, c_cg, yc)
            outs["gl_" + tag].append(s)

        merged = _merge(ya, yb, yc, w_branch, l, proj, c_m)
        out = _matmul(merged, w_out, l, "out_proj")
        if l < depth - 1:
            x, z = _postnorm(xs, out, norm_post[l], mp, ms, norm_pre[l + 1])
            xs = (x,)
        else:
            xs = _postnorm(xs, out, norm_post[l], mp, ms, None)

    yp = xs[0].reshape(bp, tp, d)
    ys = xs[1].reshape(bs, ts, d)
    st = lambda k: jnp.stack(outs[k])
    return (yp, ys, st("hg_p"), st("hg_s"), st("lh_p"), st("lh_s"),
            st("lc_p"), st("lc_s"), st("gl_p"), st("gl_s"))
```

```python
import functools
import math

import jax
import jax.numpy as jnp
from jax import lax
from jax.experimental import pallas as pl
from jax.experimental.pallas import tpu as pltpu

F32 = jnp.float32
BF16 = jnp.bfloat16

EPS = 1e-6
F_FLOOR = 1e-6
LRU_C = 8.0
GLA_TAU = 16.0
SUBLANES = 8
LANES = 128
LA_CHUNK = 64
V7X_VMEM_LIMIT = 60000 * 1024


def _vmem_limit(nbytes):
    return int(min(V7X_VMEM_LIMIT, max(32 * 1024 * 1024, nbytes * 5 // 4 + (4 << 20))))


def _pick(n, candidates):
    for c in candidates:
        if n % c == 0:
            return c
    return n


def _dot(a, b):
    return jnp.dot(a, b, preferred_element_type=F32)


def _dot_nt(a, b):
    return lax.dot_general(a, b, (((1,), (1,)), ((), ())), preferred_element_type=F32)


def _dot_tn(a, b):
    return lax.dot_general(a, b, (((0,), (0,)), ((), ())), preferred_element_type=F32)


def _sigmoid(x):
    return jax.nn.sigmoid(x)


def _silu(x):
    return x * jax.nn.sigmoid(x)


def _without_ref(kernel, idx):
    def wrapped(*refs):
        return kernel(*refs[:idx], *refs[idx + 1:])
    return wrapped


def _alias_spec(y_prev):
    return [] if y_prev is None else [pl.BlockSpec(memory_space=pl.ANY)]


def _alias_arg(y_prev):
    return [] if y_prev is None else [y_prev]


def _alias_map(y_prev, idx):
    return {} if y_prev is None else {idx: 0}


_ARB2 = ("arbitrary", "arbitrary")
_ARB3 = ("arbitrary", "arbitrary", "arbitrary")


def _rms(x, w):
    return x * lax.rsqrt(jnp.mean(x * x, axis=-1, keepdims=True) + EPS) * w


def _group_specs(mp, ms, d):
    tr = next(c for c in (256, 128, 64, 32, 16, 8) if mp % c == 0 and ms % c == 0)
    nbp = mp // tr
    pspec = pl.BlockSpec((tr, d), lambda i: (jnp.minimum(i, nbp - 1), 0))
    sspec = pl.BlockSpec((tr, d), lambda i: (jnp.maximum(i - nbp, 0), 0))
    return tr, nbp, pspec, sspec


def _prenorm2_kernel(xp_ref, xs_ref, w_ref, z_ref, *, nbp):
    x = jnp.where(pl.program_id(0) < nbp, xp_ref[...], xs_ref[...])
    z_ref[...] = _rms(x, w_ref[...]).astype(z_ref.dtype)


def _prenorm(xs, w):
    d = w.shape[0]
    mp, ms = xs[0].shape[0], xs[1].shape[0]
    tr, nbp, pspec, sspec = _group_specs(mp, ms, d)
    return pl.pallas_call(
        functools.partial(_prenorm2_kernel, nbp=nbp),
        grid=((mp + ms) // tr,),
        in_specs=[pspec, sspec, pl.BlockSpec((1, d), lambda i: (0, 0))],
        out_specs=pl.BlockSpec((tr, d), lambda i: (i, 0)),
        out_shape=jax.ShapeDtypeStruct((mp + ms, d), BF16),
        compiler_params=pltpu.CompilerParams(
            dimension_semantics=("arbitrary",), vmem_limit_bytes=_vmem_limit(2 * 3 * tr * d * 4)),
        name="prenorm",
    )(xs[0], xs[1], w.reshape(1, d))


def _postnorm_kernel(*refs, nbp, n_in, last):
    i = pl.program_id(0)
    o_ref, w_ref = refs[n_in], refs[n_in + 1]
    x = refs[0][...] if n_in == 1 else jnp.where(i < nbp, refs[0][...], refs[1][...])
    y = x + _rms(o_ref[...], w_ref[...])
    if last:
        yp_ref, ys_ref = refs[n_in + 2:]

        @pl.when(i < nbp)
        def _():
            yp_ref[...] = y

        @pl.when(i >= nbp)
        def _():
            ys_ref[...] = y
    else:
        wn_ref, y_ref, z_ref = refs[n_in + 2:]
        y_ref[...] = y
        z_ref[...] = _rms(y, wn_ref[...]).astype(z_ref.dtype)


def _postnorm(xs, out, w, mp, ms, w_next):
    d = w.shape[0]
    tr, nbp, pspec, sspec = _group_specs(mp, ms, d)
    full = pl.BlockSpec((tr, d), lambda i: (i, 0))
    vec = pl.BlockSpec((1, d), lambda i: (0, 0))
    in_specs = ([pspec, sspec] if len(xs) == 2 else [full]) + [full, vec]
    args = [*xs, out, w.reshape(1, d)]
    if w_next is None:
        out_specs = [pspec, sspec]
        out_shape = [jax.ShapeDtypeStruct((mp, d), F32), jax.ShapeDtypeStruct((ms, d), F32)]
    else:
        in_specs.append(vec)
        args.append(w_next.reshape(1, d))
        out_specs = [full, full]
        out_shape = [jax.ShapeDtypeStruct((mp + ms, d), F32), jax.ShapeDtypeStruct((mp + ms, d), BF16)]
    return pl.pallas_call(
        functools.partial(_postnorm_kernel, nbp=nbp, n_in=len(xs), last=w_next is None),
        grid=((mp + ms) // tr,),
        in_specs=in_specs, out_specs=out_specs, out_shape=out_shape,
        compiler_params=pltpu.CompilerParams(
            dimension_semantics=("arbitrary",),
            vmem_limit_bytes=_vmem_limit(2 * (len(in_specs) + len(out_specs)) * tr * d * 4)),
        name="postnorm",
    )(*args)


def _mm_kernel(a_ref, w_ref, o_ref):
    o_ref[...] = _dot(a_ref[...], w_ref[...].astype(BF16))


def _mm_t_kernel(a_ref, wt_ref, ws_ref, o_ref, os_ref):
    o_ref[...] = _dot_nt(a_ref[...], wt_ref[0].astype(BF16))

    @pl.when(pl.program_id(1) == 0)
    def _():
        os_ref[...] = _dot_nt(a_ref[...], ws_ref[0].astype(BF16))


BIG_TM = (2112, 1408, 1056, 1024, 512, 320, 256, 128, 64, 32, 16)


def _matmul(a, w, layer, name):
    m, k = a.shape
    n = w.shape[2]
    tm = _pick(m, BIG_TM)
    tn = _pick(n, (512, 256, 128))
    need = tm * k * 2 + 2 * k * tn * 4 + k * tn * 2 + 3 * tm * tn * 4
    return pl.pallas_call(
        _mm_kernel,
        grid=(m // tm, n // tn),
        in_specs=[pl.BlockSpec((tm, k), lambda i, j: (i, 0), pipeline_mode=pl.Buffered(1)),
                  pl.BlockSpec((None, k, tn), lambda i, j: (layer, 0, j))],
        out_specs=pl.BlockSpec((tm, tn), lambda i, j: (i, j)),
        out_shape=jax.ShapeDtypeStruct((m, n), F32),
        compiler_params=pltpu.CompilerParams(
            dimension_semantics=_ARB2, vmem_limit_bytes=_vmem_limit(need)),
        name=name,
    )(a, w)


def _matmul_t(a, wt, layer, skip_start, skip, name):
    m, k = a.shape
    n_out = wt.shape[1] - skip
    tm = _pick(m, BIG_TM)
    tn = next(c for c in (512, 256, 128) if n_out % c == 0 and skip_start % c == 0)
    assert skip % SUBLANES == 0

    def w_index(i, j):
        row = j * tn + jnp.where(j * tn >= skip_start, skip, 0)
        return layer, pl.multiple_of(row, SUBLANES), 0

    need = tm * k * 2 + 2 * k * tn * 4 + k * tn * 2 + 3 * tm * tn * 4 + 2 * tm * LANES * 4
    return pl.pallas_call(
        _mm_t_kernel,
        grid=(m // tm, n_out // tn),
        in_specs=[pl.BlockSpec((tm, k), lambda i, j: (i, 0), pipeline_mode=pl.Buffered(1)),
                  pl.BlockSpec((pl.Element(1), pl.Element(tn), pl.Element(k)), w_index),
                  pl.BlockSpec((pl.Element(1), pl.Element(skip), pl.Element(k)),
                               lambda i, j: (layer, skip_start, 0))],
        out_specs=[pl.BlockSpec((tm, tn), lambda i, j: (i, j)),
                   pl.BlockSpec((tm, skip), lambda i, j: (i, 0))],
        out_shape=[jax.ShapeDtypeStruct((m, n_out), F32), jax.ShapeDtypeStruct((m, skip), F32)],
        compiler_params=pltpu.CompilerParams(
            dimension_semantics=_ARB2, vmem_limit_bytes=_vmem_limit(need)),
        name=name,
    )(a, wt, wt)


def _tile_row_index(n, d):
    return lax.broadcasted_iota(jnp.int32, (n, d), 0) & (SUBLANES - 1)


def _tile_roll(x, s):
    n, d = x.shape
    return pltpu.roll(x.reshape(n // SUBLANES, SUBLANES, d), s, 1).reshape(n, d)


def _tile_cumsum(g, rowi):
    c = g
    s = 1
    while s < SUBLANES:
        c = c + jnp.where(rowi >= s, _tile_roll(c, s), 0.0)
        s *= 2
    return c


def _tile_row_bcast(x, r):
    n, d = x.shape
    return jnp.concatenate(
        [jnp.broadcast_to(x[SUBLANES * i + r:SUBLANES * i + r + 1, :], (SUBLANES, d))
         for i in range(n // SUBLANES)], axis=0)


def _chunk_prepare(q, k, v, g, g_min):
    L, dk = q.shape
    nt = L // SUBLANES
    rowk = _tile_row_index(L, dk)

    c = _tile_cumsum(g, rowk)
    tot = [c[SUBLANES * i + SUBLANES - 1:SUBLANES * (i + 1), :] for i in range(nt)]
    qt = q * jnp.exp(c)
    knew = k * jnp.exp(_tile_row_bcast(c, SUBLANES - 1) - c)

    row = lax.broadcasted_iota(jnp.int32, (L, L), 0)
    col = lax.broadcasted_iota(jnp.int32, (L, L), 1)
    offs = jnp.where((row >> 3) == (col >> 3), row - col, -1)
    half = SUBLANES // 2
    if g_min is not None and -g_min * half < 80.0:
        cm = c - _tile_row_bcast(c, half - 1)
        inner = _dot_nt((q * jnp.exp(cm)).astype(BF16), (k * jnp.exp(-cm)).astype(BF16))
        band = jnp.where(offs >= 0, inner, 0.0)
    else:
        eg = jnp.exp(g)
        w = k
        band = jnp.where(offs == 0, jnp.sum(q * w, axis=1, keepdims=True), 0.0)
        for d in range(1, SUBLANES):
            w = _tile_roll(w, 1) * eg
            band = jnp.where(offs == d, jnp.sum(q * w, axis=1, keepdims=True), band)

    tiles = []
    arows = [jnp.zeros((SUBLANES, L), F32)]
    qhat = [qt[0:SUBLANES]]
    er = None
    for i in range(nt):
        lo, hi = SUBLANES * i, SUBLANES * (i + 1)
        if i > 0:
            pad = jnp.zeros((L - lo, dk), F32)
            kh = jnp.concatenate(tiles + [pad], axis=0).astype(BF16)
            arows.append(_dot_nt(qt[lo:hi].astype(BF16), kh))
            qhat.append(qt[lo:hi] * er)
        dcy = jnp.exp(tot[i])
        tiles = [t * dcy for t in tiles] + [knew[lo:hi]]
        er = dcy if er is None else er * dcy

    khat = jnp.concatenate(tiles, axis=0).astype(BF16)
    qh = jnp.concatenate(qhat, axis=0).astype(BF16)
    scores = (jnp.concatenate(arows, axis=0) + band).astype(BF16)
    return scores, v.astype(BF16), qh, khat, er


def _chunk_finish(scores, vb, qh, khat, er, st):
    o = _dot(scores, vb) + _dot_nt(qh, st.astype(BF16))
    return o, st * er + _dot_tn(vb, khat)


class _ChunkScratch:
    def __init__(self, refs):
        self.sc, self.vb, self.qh, self.kh, self.er = refs

    @staticmethod
    def shapes(nh, chunk, dk, dv):
        return [pltpu.VMEM((nh, chunk, chunk), BF16), pltpu.VMEM((nh, chunk, dv), BF16),
                pltpu.VMEM((nh, chunk, dk), BF16), pltpu.VMEM((nh, chunk, dk), BF16),
                pltpu.VMEM((nh, SUBLANES, dk), F32)]

    def put(self, h, scores, vb, qh, khat, er):
        self.sc[h] = scores
        self.vb[h] = vb
        self.qh[h] = qh
        self.kh[h] = khat
        self.er[h] = jnp.broadcast_to(er, self.er.shape[1:])

    def get(self, h):
        return self.sc[h], self.vb[h], self.qh[h], self.kh[h], self.er[h][0:1, :]


def _pipelined_chunks(n, prepare, finish):
    prepare(0)

    def body(ci, carry):
        finish(ci - 1)
        prepare(ci)
        return carry

    lax.fori_loop(1, n, body, 0)
    finish(n - 1)


def _sequential_chunks(n, prepare, finish):
    def body(ci, carry):
        prepare(ci)
        finish(ci)
        return carry

    lax.fori_loop(0, n, body, 0)


def _head_norm_gate(o, w, gate):
    on = o * lax.rsqrt(jnp.mean(o * o, axis=-1, keepdims=True) + EPS)
    return on * w * _silu(gate)


def _chunk_rows(ci, chunk):
    start = ci * chunk
    return pl.ds(start if isinstance(start, int) else pl.multiple_of(start, chunk), chunk)


def _hgrn_kernel(q_ref, f_ref, i_ref, gate_ref, lb_ref, nw_ref, s0_ref,
                 y_ref, sout_ref, st_ref, *scratch, chunk):
    t = pl.program_id(2)
    tb = q_ref.shape[0]
    nh, dk, dv = s0_ref.shape
    hand = _ChunkScratch(scratch)
    g_min = math.log(F_FLOOR) - 1e-3

    @pl.when(t == 0)
    def _():
        for h in range(nh):
            st_ref[h] = s0_ref[h].T

    def prepare(ci):
        rows = _chunk_rows(ci, chunk)
        for h in range(nh):
            cols = slice(h * dk, (h + 1) * dk)
            lb = lb_ref[:, cols]
            f = lb + (1.0 - lb) * _sigmoid(f_ref[rows, cols])
            g = jnp.log(jnp.maximum(f, F_FLOOR))
            hand.put(h, *_chunk_prepare(_silu(q_ref[rows, cols]), 1.0 - f, i_ref[rows, cols], g, g_min))

    def finish(ci):
        rows = _chunk_rows(ci, chunk)
        for h in range(nh):
            cols = slice(h * dk, (h + 1) * dk)
            o, st_ref[h] = _chunk_finish(*hand.get(h), st_ref[h])
            y = _head_norm_gate(o, nw_ref[:, cols], gate_ref[rows, cols])
            y_ref[rows, cols] = y.astype(y_ref.dtype)

    _pipelined_chunks(tb // chunk, prepare, finish)

    @pl.when(t == pl.num_programs(2) - 1)
    def _():
        for h in range(nh):
            sout_ref[h] = st_ref[h].T


def _hgrn(proj, row0, nb, t_len, lb, hg_norm, s0, col0, y_prev):
    _, nh, dk, dv = s0.shape
    w = nh * dk
    hpb = nh if t_len <= LA_CHUNK else _pick(nh, (8, 4, 2, 1))
    bw = hpb * dk
    tb = _pick(t_len, (1024, 512, 256, 128, 64, 32))
    chunk = min(LA_CHUNK, tb)
    nt = t_len // tb
    rb0 = row0 // tb

    def col(g):
        return pl.BlockSpec((tb, bw), lambda b, h, t: (rb0 + b * nt + t, (col0 + g * w) // bw + h))

    vec = pl.BlockSpec((1, bw), lambda b, h, t: (0, h))
    st = pl.BlockSpec((None, hpb, dk, dv), lambda b, h, t: (b, h, 0, 0))
    kern = functools.partial(_hgrn_kernel, chunk=chunk)
    args = [proj, proj, proj, proj, lb.reshape(1, w), hg_norm.reshape(1, w), s0]
    y, s_new = pl.pallas_call(
        kern if y_prev is None else _without_ref(kern, len(args)),
        grid=(nb, nh // hpb, nt),
        in_specs=[col(0), col(1), col(2), col(3), vec, vec, st] + _alias_spec(y_prev),
        out_specs=[pl.BlockSpec((tb, bw), lambda b, h, t: (rb0 + b * nt + t, h)), st],
        out_shape=[jax.ShapeDtypeStruct((proj.shape[0], w), BF16),
                   jax.ShapeDtypeStruct(s0.shape, F32)],
        scratch_shapes=[pltpu.VMEM((hpb, dv, dk), F32)] + _ChunkScratch.shapes(hpb, chunk, dk, dv),
        input_output_aliases=_alias_map(y_prev, len(args)),
        compiler_params=pltpu.CompilerParams(
            dimension_semantics=_ARB3,
            vmem_limit_bytes=_vmem_limit(2 * tb * bw * (4 * 4 + 2) + 5 * hpb * dk * dv * 4)),
        name="hgrn2",
    )(*args, *_alias_arg(y_prev))
    return y, s_new


def _gla_kernel(q_ref, k_ref, v_ref, gate_ref, lr_ref, w2_ref, b2_ref, nw_ref, s0_ref,
                y_ref, sout_ref, st_ref, *scratch, chunk, scale):
    t = pl.program_id(2)
    tb = q_ref.shape[0]
    nh, dk, dv = s0_ref.shape
    hand = _ChunkScratch(scratch)

    @pl.when(t == 0)
    def _():
        for h in range(nh):
            st_ref[h] = s0_ref[h].T

    def prepare(ci):
        rows = _chunk_rows(ci, chunk)
        logits = _dot(lr_ref[rows, :].astype(BF16), w2_ref[...].astype(BF16)) + b2_ref[...]
        g = (jnp.minimum(logits, 0.0) - jnp.log1p(jnp.exp(-jnp.abs(logits)))) / GLA_TAU
        for h in range(nh):
            kc, vc = slice(h * dk, (h + 1) * dk), slice(h * dv, (h + 1) * dv)
            hand.put(h, *_chunk_prepare(q_ref[rows, kc] * scale, k_ref[rows, kc], v_ref[rows, vc],
                                        g[:, kc], None))

    def finish(ci):
        rows = _chunk_rows(ci, chunk)
        for h in range(nh):
            vc = slice(h * dv, (h + 1) * dv)
            o, st_ref[h] = _chunk_finish(*hand.get(h), st_ref[h])
            y_ref[rows, vc] = _head_norm_gate(o, nw_ref[:, vc], gate_ref[rows, vc]).astype(y_ref.dtype)

    _sequential_chunks(tb // chunk, prepare, finish)

    @pl.when(t == pl.num_programs(2) - 1)
    def _():
        for h in range(nh):
            sout_ref[h] = st_ref[h].T


def _gla(proj, lr, row0, nb, t_len, w2, b2, gla_norm, s0, col_q, col_k, col_v, col_g, y_prev):
    _, nh, dk, dv = s0.shape
    rank = lr.shape[1]
    hpb = nh if t_len <= LA_CHUNK else _pick(nh, (4, 2, 1))
    kw, vw = hpb * dk, hpb * dv
    tb = _pick(t_len, (512, 256, 128, 64, 32))
    chunk = min(LA_CHUNK, tb)
    nt = t_len // tb
    rb0 = row0 // tb

    def col(c0, wd):
        return pl.BlockSpec((tb, wd), lambda b, h, t: (rb0 + b * nt + t, c0 // wd + h))

    st = pl.BlockSpec((None, hpb, dk, dv), lambda b, h, t: (b, h, 0, 0))
    kern = functools.partial(_gla_kernel, chunk=chunk, scale=float(dk) ** -0.5)
    args = [proj, proj, proj, proj, lr, w2, b2.reshape(1, nh * dk),
            gla_norm.reshape(1, nh * dv), s0]
    y, s_new = pl.pallas_call(
        kern if y_prev is None else _without_ref(kern, len(args)),
        grid=(nb, nh // hpb, nt),
        in_specs=[col(col_q, kw), col(col_k, kw), col(col_v, vw), col(col_g, vw),
                  pl.BlockSpec((tb, rank), lambda b, h, t: (rb0 + b * nt + t, 0)),
                  pl.BlockSpec((rank, kw), lambda b, h, t: (0, h)),
                  pl.BlockSpec((1, kw), lambda b, h, t: (0, h)),
                  pl.BlockSpec((1, vw), lambda b, h, t: (0, h)),
                  st] + _alias_spec(y_prev),
        out_specs=[pl.BlockSpec((tb, vw), lambda b, h, t: (rb0 + b * nt + t, h)), st],
        out_shape=[jax.ShapeDtypeStruct((proj.shape[0], nh * dv), BF16),
                   jax.ShapeDtypeStruct(s0.shape, F32)],
        scratch_shapes=[pltpu.VMEM((hpb, dv, dk), F32)] + _ChunkScratch.shapes(hpb, chunk, dk, dv),
        input_output_aliases=_alias_map(y_prev, len(args)),
        compiler_params=pltpu.CompilerParams(
            dimension_semantics=_ARB3,
            vmem_limit_bytes=_vmem_limit(2 * tb * (2 * kw * 4 + 2 * vw * 4 + vw * 2) + 5 * hpb * dk * dv * 4)),
        name="gla",
    )(*args, *_alias_arg(y_prev))
    return y, s_new


def _lru_kernel(x_ref, gate_ref, cs_ref, h0_ref, cw_ref, cb_ref, wa_ref, ba_ref, wx_ref, bx_ref,
                lam_ref, y_ref, hout_ref, cout_ref, hist_ref, hcar_ref, a_ref, u_ref, h_ref):
    t = pl.program_id(2)
    tb, wd = x_ref.shape
    ncw = cw_ref.shape[0]
    nblk, bd, _ = wa_ref.shape

    @pl.when(t == 0)
    def _():
        hist_ref[...] = jnp.zeros_like(hist_ref)
        hist_ref[SUBLANES - (ncw - 1):SUBLANES, :] = cs_ref[...]
        hcar_ref[...] = jnp.broadcast_to(h0_ref[...], (SUBLANES, wd))

    x = x_ref[...]
    xe = jnp.concatenate([hist_ref[...], x], axis=0)
    cw = cw_ref[...]
    xc = cb_ref[...] + x * cw[ncw - 1:ncw, :]
    for s in range(1, ncw):
        xc = xc + pltpu.roll(xe, s, 0)[SUBLANES:, :] * cw[ncw - 1 - s:ncw - s, :]
    hist_ref[...] = x[tb - SUBLANES:tb, :]

    xb = xc.astype(BF16)
    ra, ri = [], []
    for j in range(nblk):
        xj = xb[:, j * bd:(j + 1) * bd]
        ra.append(_dot(xj, wa_ref[j].astype(BF16)))
        ri.append(_dot(xj, wx_ref[j].astype(BF16)))
    r = _sigmoid(jnp.concatenate(ra, axis=1) + ba_ref[...])
    gi = _sigmoid(jnp.concatenate(ri, axis=1) + bx_ref[...])
    lam = lam_ref[...]
    softplus_neg = jnp.maximum(-lam, 0.0) + jnp.log1p(jnp.exp(-jnp.abs(lam)))
    log_a = -LRU_C * r * softplus_neg
    a = jnp.exp(log_a)
    x2 = 2.0 * log_a
    e2 = jnp.exp(x2)
    small = jnp.where(e2 == 1.0, -x2, (1.0 - e2) * x2 / jnp.log(jnp.where(e2 > 0.25, e2, 0.5)))
    one_m_a2 = jnp.where(x2 < -0.5, 1.0 - e2, small)
    u = jnp.sqrt(jnp.maximum(one_m_a2, 0.0)) * (gi * xc)

    rowi = _tile_row_index(tb, wd)
    s = 1
    while s < SUBLANES:
        ok = rowi >= s
        u = u + jnp.where(ok, a * _tile_roll(u, s), 0.0)
        a = a * jnp.where(ok, _tile_roll(a, s), 1.0)
        s *= 2
    a_ref[...] = a
    u_ref[...] = u

    def body(i, hp):
        rows = pl.ds(pl.multiple_of(i * SUBLANES, SUBLANES), SUBLANES)
        h = u_ref[rows, :] + a_ref[rows, :] * hp
        h_ref[rows, :] = h
        return jnp.broadcast_to(h[SUBLANES - 1:SUBLANES, :], (SUBLANES, wd))

    hp = lax.fori_loop(0, tb // SUBLANES, body, hcar_ref[...], unroll=4)
    hcar_ref[...] = hp
    y_ref[...] = (h_ref[...] * _silu(gate_ref[...])).astype(y_ref.dtype)

    @pl.when(t == pl.num_programs(2) - 1)
    def _():
        hout_ref[...] = hp[0:1, :]
        cout_ref[...] = x[tb - (ncw - 1):tb, :]


def _lru(proj, row0, nb, t_len, conv_state, h0, conv_w, conv_b, wa, ba, wx, bx, lam, col_x, col_g, y_prev):
    nblk, bd, _ = wa.shape
    w = nblk * bd
    ncw = conv_w.shape[0]
    bps = nblk if t_len <= LA_CHUNK else _pick(nblk, (4, 2, 1))
    bw = bps * bd
    tb = _pick(t_len, (512, 256, 128, 64, 32))
    nt = t_len // tb
    rb0 = row0 // tb

    def col(c0):
        return pl.BlockSpec((tb, bw), lambda b, h, t: (rb0 + b * nt + t, c0 // bw + h))

    vec = pl.BlockSpec((1, bw), lambda b, h, t: (0, h))
    blk = pl.BlockSpec((bps, bd, bd), lambda b, h, t: (h, 0, 0))
    hspec = pl.BlockSpec((None, 1, bw), lambda b, h, t: (b, 0, h))
    cspec = pl.BlockSpec((None, ncw - 1, bw), lambda b, h, t: (b, 0, h))
    args = [proj, proj, conv_state, h0.reshape(nb, 1, w), conv_w, conv_b.reshape(1, w),
            wa, ba.reshape(1, w), wx, bx.reshape(1, w), lam.reshape(1, w)]
    y, h_new, c_new = pl.pallas_call(
        _lru_kernel if y_prev is None else _without_ref(_lru_kernel, len(args)),
        grid=(nb, nblk // bps, nt),
        in_specs=[col(col_x), col(col_g), cspec, hspec,
                  pl.BlockSpec((ncw, bw), lambda b, h, t: (0, h)), vec,
                  blk, vec, blk, vec, vec] + _alias_spec(y_prev),
        out_specs=[pl.BlockSpec((tb, bw), lambda b, h, t: (rb0 + b * nt + t, h)), hspec, cspec],
        out_shape=[jax.ShapeDtypeStruct((proj.shape[0], w), BF16),
                   jax.ShapeDtypeStruct((nb, 1, w), F32),
                   jax.ShapeDtypeStruct((nb, ncw - 1, w), F32)],
        scratch_shapes=[pltpu.VMEM((SUBLANES, bw), F32), pltpu.VMEM((SUBLANES, bw), F32),
                        pltpu.VMEM((tb, bw), F32), pltpu.VMEM((tb, bw), F32),
                        pltpu.VMEM((tb, bw), F32)],
        input_output_aliases=_alias_map(y_prev, len(args)),
        compiler_params=pltpu.CompilerParams(dimension_semantics=_ARB3),
        name="rglru",
    )(*args, *_alias_arg(y_prev))
    return y, h_new.reshape(nb, w), c_new


def _merge_kernel(ya_ref, yb_ref, yc_ref, w_ref, m0_ref, m1_ref, m2_ref, o_ref):
    acc = _sigmoid(m0_ref[...]) * _dot(ya_ref[...], w_ref[0].astype(BF16))
    acc = acc + _sigmoid(m1_ref[...]) * _dot(yb_ref[...], w_ref[1].astype(BF16))
    acc = acc + _sigmoid(m2_ref[...]) * _dot(yc_ref[...], w_ref[2].astype(BF16))
    o_ref[...] = acc.astype(o_ref.dtype)


def _merge(ya, yb, yc, wbr, layer, proj, col_m):
    m, bw = ya.shape
    _, nbr, _, d = wbr.shape
    tm = _pick(m, (1408, 1056, 1024, 512, 320, 256, 128, 64, 32, 16))
    tn = next(c for c in (256, 128) if d % c == 0 and col_m % c == 0)
    yspec = pl.BlockSpec((tm, bw), lambda i, j: (i, 0), pipeline_mode=pl.Buffered(1))

    def mg(n):
        return pl.BlockSpec((tm, tn), lambda i, j: (i, (col_m + n * d) // tn + j))

    need = (3 * tm * bw * 2 + 2 * nbr * bw * tn * 4 + nbr * bw * tn * 2
            + 2 * 3 * tm * tn * 4 + 6 * tm * tn * 4)
    return pl.pallas_call(
        _merge_kernel,
        grid=(m // tm, d // tn),
        in_specs=[yspec, yspec, yspec,
                  pl.BlockSpec((None, nbr, bw, tn), lambda i, j: (layer, 0, 0, j)),
                  mg(0), mg(1), mg(2)],
        out_specs=pl.BlockSpec((tm, tn), lambda i, j: (i, j)),
        out_shape=jax.ShapeDtypeStruct((m, d), BF16),
        compiler_params=pltpu.CompilerParams(
            dimension_semantics=_ARB2, vmem_limit_bytes=_vmem_limit(need)),
        name="merge",
    )(ya, yb, yc, wbr, proj, proj, proj)


def kernel(x_prompt, x_sample, state_hgrn, state_lru_h, state_lru_conv, state_gla, norm_pre, norm_post, w_in, hg_lb_logits, hg_norm, lru_conv_w, lru_conv_b, lru_wa, lru_ba, lru_wx, lru_bx, lru_lambda, gla_w2, gla_b2, gla_norm, w_branch, w_out):
    bp, tp, d = x_prompt.shape
    bs, ts, _ = x_sample.shape
    depth, _, in_cols = w_in.shape
    _, _, hg_h, hg_dk, hg_dv = state_hgrn.shape
    hg_w = hg_h * hg_dk
    lru_w = state_lru_h.shape[-1]
    _, _, gl_h, gl_dk, gl_dv = state_gla.shape
    gl_kw, gl_vw = gl_h * gl_dk, gl_h * gl_dv
    rank = gla_w2.shape[1]
    dt = x_prompt.dtype

    c_hg = 0
    c_lx = 4 * hg_w
    c_lg = c_lx + lru_w
    c_q = c_lg + lru_w
    c_k = c_q + gl_kw
    c_v = c_k + gl_kw
    c_lr = c_v + gl_vw
    c_cg = c_lr
    c_m = c_cg + gl_vw
    n_main = in_cols - rank
    assert c_m + 3 * d == n_main

    lb_sm = jax.nn.softmax(hg_lb_logits.astype(F32), axis=0)
    lb_all = jnp.cumsum(lb_sm, axis=0) - lb_sm[0:1]
    w_in_t = jnp.transpose(w_in, (0, 2, 1))

    mp, ms = bp * tp, bs * ts
    xs = (x_prompt.reshape(mp, d), x_sample.reshape(ms, d))

    zeros = lambda shape: jnp.zeros(shape, dt)
    outs = {k: [] for k in ("hg_p", "hg_s", "lh_p", "lh_s", "lc_p", "lc_s", "gl_p", "gl_s")}
    z = _prenorm(xs, norm_pre[0])
    for l in range(depth):
        proj, lr = _matmul_t(z, w_in_t, l, c_lr, rank, "in_proj")

        groups = (
            ("p", 0, bp, tp, zeros((bp,) + state_hgrn.shape[2:]), zeros((bp, lru_w)),
             zeros((bp,) + state_lru_conv.shape[2:]), zeros((bp,) + state_gla.shape[2:])),
            ("s", mp, bs, ts, state_hgrn[l], state_lru_h[l], state_lru_conv[l], state_gla[l]),
        )
        ya = jnp.zeros((mp + ms, hg_w), BF16)
        yb = jnp.zeros((mp + ms, lru_w), BF16)
        yc = jnp.zeros((mp + ms, gl_vw), BF16)
        for tag, row0, nb, t_len, s_hg, s_lh, s_lc, s_gl in groups:
            ya, s = _hgrn(proj, row0, nb, t_len, lb_all[l], hg_norm[l], s_hg, c_hg, ya)
            outs["hg_" + tag].append(s)
            yb, hn, cn = _lru(proj, row0, nb, t_len, s_lc, s_lh, lru_conv_w[l], lru_conv_b[l],
                              lru_wa[l], lru_ba[l], lru_wx[l], lru_bx[l], lru_lambda[l], c_lx, c_lg, yb)
            outs["lh_" + tag].append(hn); outs["lc_" + tag].append(cn)
            yc, s = _gla(proj, lr, row0, nb, t_len, gla_w2[l], gla_b2[l], gla_norm[l], s_gl,
                         c_q, c_k, c_v, c_cg, yc)
            outs["gl_" + tag].append(s)

        merged = _merge(ya, yb, yc, w_branch, l, proj, c_m)
        out = _matmul(merged, w_out, l, "out_proj")
        if l < depth - 1:
            x, z = _postnorm(xs, out, norm_post[l], mp, ms, norm_pre[l + 1])
            xs = (x,)
        else:
            xs = _postnorm(xs, out, norm_post[l], mp, ms, None)

    yp = xs[0].reshape(bp, tp, d)
    ys = xs[1].reshape(bs, ts, d)
    st = lambda k: jnp.stack(outs[k])
    return (yp, ys, st("hg_p"), st("hg_s"), st("lh_p"), st("lh_s"),
            st("lc_p"), st("lc_s"), st("gl_p"), st("gl_s"))
```

```python
import functools
import math

import jax
import jax.numpy as jnp
from jax import lax
from jax.experimental import pallas as pl
from jax.experimental.pallas import tpu as pltpu

F32 = jnp.float32
BF16 = jnp.bfloat16

EPS = 1e-6
F_FLOOR = 1e-6
LRU_C = 8.0
GLA_TAU = 16.0
SUBLANES = 8
LANES = 128
LA_CHUNK = 64
V7X_VMEM_LIMIT = 60000 * 1024


def _vmem_limit(nbytes):
    return int(min(V7X_VMEM_LIMIT, max(32 * 1024 * 1024, nbytes * 5 // 4 + (4 << 20))))


def _pick(n, candidates):
    for c in candidates:
        if n % c == 0:
            return c
    return n


def _dot(a, b):
    return jnp.dot(a, b, preferred_element_type=F32)


def _dot_nt(a, b):
    return lax.dot_general(a, b, (((1,), (1,)), ((), ())), preferred_element_type=F32)


def _dot_tn(a, b):
    return lax.dot_general(a, b, (((0,), (0,)), ((), ())), preferred_element_type=F32)


def _sigmoid(x):
    return jax.nn.sigmoid(x)


def _silu(x):
    return x * jax.nn.sigmoid(x)


def _without_ref(kernel, idx):
    def wrapped(*refs):
        return kernel(*refs[:idx], *refs[idx + 1:])
    return wrapped


def _alias_spec(y_prev):
    return [] if y_prev is None else [pl.BlockSpec(memory_space=pl.ANY)]


def _alias_arg(y_prev):
    return [] if y_prev is None else [y_prev]


def _alias_map(y_prev, idx):
    return {} if y_prev is None else {idx: 0}


_ARB2 = ("arbitrary", "arbitrary")
_ARB3 = ("arbitrary", "arbitrary", "arbitrary")


def _rms(x, w):
    return x * lax.rsqrt(jnp.mean(x * x, axis=-1, keepdims=True) + EPS) * w


def _group_specs(mp, ms, d):
    tr = next(c for c in (256, 128, 64, 32, 16, 8) if mp % c == 0 and ms % c == 0)
    nbp = mp // tr
    pspec = pl.BlockSpec((tr, d), lambda i: (jnp.minimum(i, nbp - 1), 0))
    sspec = pl.BlockSpec((tr, d), lambda i: (jnp.maximum(i - nbp, 0), 0))
    return tr, nbp, pspec, sspec


def _prenorm2_kernel(xp_ref, xs_ref, w_ref, z_ref, *, nbp):
    x = jnp.where(pl.program_id(0) < nbp, xp_ref[...], xs_ref[...])
    z_ref[...] = _rms(x, w_ref[...]).astype(z_ref.dtype)


def _prenorm(xs, w):
    d = w.shape[0]
    mp, ms = xs[0].shape[0], xs[1].shape[0]
    tr, nbp, pspec, sspec = _group_specs(mp, ms, d)
    return pl.pallas_call(
        functools.partial(_prenorm2_kernel, nbp=nbp),
        grid=((mp + ms) // tr,),
        in_specs=[pspec, sspec, pl.BlockSpec((1, d), lambda i: (0, 0))],
        out_specs=pl.BlockSpec((tr, d), lambda i: (i, 0)),
        out_shape=jax.ShapeDtypeStruct((mp + ms, d), BF16),
        compiler_params=pltpu.CompilerParams(
            dimension_semantics=("arbitrary",), vmem_limit_bytes=_vmem_limit(2 * 3 * tr * d * 4)),
        name="prenorm",
    )(xs[0], xs[1], w.reshape(1, d))


def _postnorm_kernel(*refs, nbp, n_in, last):
    i = pl.program_id(0)
    o_ref, w_ref = refs[n_in], refs[n_in + 1]
    x = refs[0][...] if n_in == 1 else jnp.where(i < nbp, refs[0][...], refs[1][...])
    y = x + _rms(o_ref[...], w_ref[...])
    if last:
        yp_ref, ys_ref = refs[n_in + 2:]

        @pl.when(i < nbp)
        def _():
            yp_ref[...] = y

        @pl.when(i >= nbp)
        def _():
            ys_ref[...] = y
    else:
        wn_ref, y_ref, z_ref = refs[n_in + 2:]
        y_ref[...] = y
        z_ref[...] = _rms(y, wn_ref[...]).astype(z_ref.dtype)


def _postnorm(xs, out, w, mp, ms, w_next):
    d = w.shape[0]
    tr, nbp, pspec, sspec = _group_specs(mp, ms, d)
    full = pl.BlockSpec((tr, d), lambda i: (i, 0))
    vec = pl.BlockSpec((1, d), lambda i: (0, 0))
    in_specs = ([pspec, sspec] if len(xs) == 2 else [full]) + [full, vec]
    args = [*xs, out, w.reshape(1, d)]
    if w_next is None:
        out_specs = [pspec, sspec]
        out_shape = [jax.ShapeDtypeStruct((mp, d), F32), jax.ShapeDtypeStruct((ms, d), F32)]
    else:
        in_specs.append(vec)
        args.append(w_next.reshape(1, d))
        out_specs = [full, full]
        out_shape = [jax.ShapeDtypeStruct((mp + ms, d), F32), jax.ShapeDtypeStruct((mp + ms, d), BF16)]
    return pl.pallas_call(
        functools.partial(_postnorm_kernel, nbp=nbp, n_in=len(xs), last=w_next is None),
        grid=((mp + ms) // tr,),
        in_specs=in_specs, out_specs=out_specs, out_shape=out_shape,
        compiler_params=pltpu.CompilerParams(
            dimension_semantics=("arbitrary",),
            vmem_limit_bytes=_vmem_limit(2 * (len(in_specs) + len(out_specs)) * tr * d * 4)),
        name="postnorm",
    )(*args)


def _mm_kernel(a_ref, w_ref, o_ref):
    o_ref[...] = _dot(a_ref[...], w_ref[...].astype(BF16))


def _mm_t_kernel(a_ref, wt_ref, ws_ref, o_ref, os_ref):
    o_ref[...] = _dot_nt(a_ref[...], wt_ref[0].astype(BF16))

    @pl.when(pl.program_id(1) == 0)
    def _():
        os_ref[...] = _dot_nt(a_ref[...], ws_ref[0].astype(BF16))


BIG_TM = (2112, 1408, 1056, 1024, 512, 320, 256, 128, 64, 32, 16)


def _matmul(a, w, layer, name):
    m, k = a.shape
    n = w.shape[2]
    tm = _pick(m, BIG_TM)
    tn = _pick(n, (512, 256, 128))
    need = tm * k * 2 + 2 * k * tn * 4 + k * tn * 2 + 3 * tm * tn * 4
    return pl.pallas_call(
        _mm_kernel,
        grid=(m // tm, n // tn),
        in_specs=[pl.BlockSpec((tm, k), lambda i, j: (i, 0), pipeline_mode=pl.Buffered(1)),
                  pl.BlockSpec((None, k, tn), lambda i, j: (layer, 0, j))],
        out_specs=pl.BlockSpec((tm, tn), lambda i, j: (i, j)),
        out_shape=jax.ShapeDtypeStruct((m, n), F32),
        compiler_params=pltpu.CompilerParams(
            dimension_semantics=_ARB2, vmem_limit_bytes=_vmem_limit(need)),
        name=name,
    )(a, w)


def _matmul_t(a, wt, layer, skip_start, skip, name):
    m, k = a.shape
    n_out = wt.shape[1] - skip
    tm = _pick(m, BIG_TM)
    tn = next(c for c in (512, 256, 128) if n_out % c == 0 and skip_start % c == 0)
    assert skip % SUBLANES == 0

    def w_index(i, j):
        row = j * tn + jnp.where(j * tn >= skip_start, skip, 0)
        return layer, pl.multiple_of(row, SUBLANES), 0

    need = tm * k * 2 + 2 * k * tn * 4 + k * tn * 2 + 3 * tm * tn * 4 + 2 * tm * LANES * 4
    return pl.pallas_call(
        _mm_t_kernel,
        grid=(m // tm, n_out // tn),
        in_specs=[pl.BlockSpec((tm, k), lambda i, j: (i, 0), pipeline_mode=pl.Buffered(1)),
                  pl.BlockSpec((pl.Element(1), pl.Element(tn), pl.Element(k)), w_index),
                  pl.BlockSpec((pl.Element(1), pl.Element(skip), pl.Element(k)),
                               lambda i, j: (layer, skip_start, 0))],
        out_specs=[pl.BlockSpec((tm, tn), lambda i, j: (i, j)),
                   pl.BlockSpec((tm, skip), lambda i, j: (i, 0))],
        out_shape=[jax.ShapeDtypeStruct((m, n_out), F32), jax.ShapeDtypeStruct((m, skip), F32)],
        compiler_params=pltpu.CompilerParams(
            dimension_semantics=_ARB2, vmem_limit_bytes=_vmem_limit(need)),
        name=name,
    )(a, wt, wt)


def _tile_row_index(n, d):
    return lax.broadcasted_iota(jnp.int32, (n, d), 0) & (SUBLANES - 1)


def _tile_roll(x, s):
    n, d = x.shape
    return pltpu.roll(x.reshape(n // SUBLANES, SUBLANES, d), s, 1).reshape(n, d)


def _tile_cumsum(g, rowi):
    c = g
    s = 1
    while s < SUBLANES:
        c = c + jnp.where(rowi >= s, _tile_roll(c, s), 0.0)
        s *= 2
    return c


def _tile_row_bcast(x, r):
    n, d = x.shape
    return jnp.concatenate(
        [jnp.broadcast_to(x[SUBLANES * i + r:SUBLANES * i + r + 1, :], (SUBLANES, d))
         for i in range(n // SUBLANES)], axis=0)


def _chunk_prepare(q, k, v, g, g_min):
    L, dk = q.shape
    nt = L // SUBLANES
    rowk = _tile_row_index(L, dk)

    c = _tile_cumsum(g, rowk)
    tot = [c[SUBLANES * i + SUBLANES - 1:SUBLANES * (i + 1), :] for i in range(nt)]
    qt = q * jnp.exp(c)
    knew = k * jnp.exp(_tile_row_bcast(c, SUBLANES - 1) - c)

    row = lax.broadcasted_iota(jnp.int32, (L, L), 0)
    col = lax.broadcasted_iota(jnp.int32, (L, L), 1)
    offs = jnp.where((row >> 3) == (col >> 3), row - col, -1)
    half = SUBLANES // 2
    if g_min is not None and -g_min * half < 80.0:
        cm = c - _tile_row_bcast(c, half - 1)
        inner = _dot_nt((q * jnp.exp(cm)).astype(BF16), (k * jnp.exp(-cm)).astype(BF16))
        band = jnp.where(offs >= 0, inner, 0.0)
    else:
        eg = jnp.exp(g)
        w = k
        band = jnp.where(offs == 0, jnp.sum(q * w, axis=1, keepdims=True), 0.0)
        for d in range(1, SUBLANES):
            w = _tile_roll(w, 1) * eg
            band = jnp.where(offs == d, jnp.sum(q * w, axis=1, keepdims=True), band)

    tiles = []
    arows = [jnp.zeros((SUBLANES, L), F32)]
    qhat = [qt[0:SUBLANES]]
    er = None
    for i in range(nt):
        lo, hi = SUBLANES * i, SUBLANES * (i + 1)
        if i > 0:
            pad = jnp.zeros((L - lo, dk), F32)
            kh = jnp.concatenate(tiles + [pad], axis=0).astype(BF16)
            arows.append(_dot_nt(qt[lo:hi].astype(BF16), kh))
            qhat.append(qt[lo:hi] * er)
        dcy = jnp.exp(tot[i])
        tiles = [t * dcy for t in tiles] + [knew[lo:hi]]
        er = dcy if er is None else er * dcy

    khat = jnp.concatenate(tiles, axis=0).astype(BF16)
    qh = jnp.concatenate(qhat, axis=0).astype(BF16)
    scores = (jnp.concatenate(arows, axis=0) + band).astype(BF16)
    return scores, v.astype(BF16), qh, khat, er


def _chunk_finish(scores, vb, qh, khat, er, st):
    o = _dot(scores, vb) + _dot_nt(qh, st.astype(BF16))
    return o, st * er + _dot_tn(vb, khat)


class _ChunkScratch:
    def __init__(self, refs):
        self.sc, self.vb, self.qh, self.kh, self.er = refs

    @staticmethod
    def shapes(nh, chunk, dk, dv):
        return [pltpu.VMEM((nh, chunk, chunk), BF16), pltpu.VMEM((nh, chunk, dv), BF16),
                pltpu.VMEM((nh, chunk, dk), BF16), pltpu.VMEM((nh, chunk, dk), BF16),
                pltpu.VMEM((nh, SUBLANES, dk), F32)]

    def put(self, h, scores, vb, qh, khat, er):
        self.sc[h] = scores
        self.vb[h] = vb
        self.qh[h] = qh
        self.kh[h] = khat
        self.er[h] = jnp.broadcast_to(er, self.er.shape[1:])

    def get(self, h):
        return self.sc[h], self.vb[h], self.qh[h], self.kh[h], self.er[h][0:1, :]


def _pipelined_chunks(n, prepare, finish):
    prepare(0)

    def body(ci, carry):
        finish(ci - 1)
        prepare(ci)
        return carry

    lax.fori_loop(1, n, body, 0)
    finish(n - 1)


def _sequential_chunks(n, prepare, finish):
    def body(ci, carry):
        prepare(ci)
        finish(ci)
        return carry

    lax.fori_loop(0, n, body, 0)


def _head_norm_gate(o, w, gate):
    on = o * lax.rsqrt(jnp.mean(o * o, axis=-1, keepdims=True) + EPS)
    return on * w * _silu(gate)


def _chunk_rows(ci, chunk):
    start = ci * chunk
    return pl.ds(start if isinstance(start, int) else pl.multiple_of(start, chunk), chunk)


def _hgrn_kernel(q_ref, f_ref, i_ref, gate_ref, lb_ref, nw_ref, s0_ref,
                 y_ref, sout_ref, st_ref, *scratch, chunk):
    t = pl.program_id(2)
    tb = q_ref.shape[0]
    nh, dk, dv = s0_ref.shape
    hand = _ChunkScratch(scratch)
    g_min = math.log(F_FLOOR) - 1e-3

    @pl.when(t == 0)
    def _():
        for h in range(nh):
            st_ref[h] = s0_ref[h].T

    def prepare(ci):
        rows = _chunk_rows(ci, chunk)
        for h in range(nh):
            cols = slice(h * dk, (h + 1) * dk)
            lb = lb_ref[:, cols]
            f = lb + (1.0 - lb) * _sigmoid(f_ref[rows, cols])
            g = jnp.log(jnp.maximum(f, F_FLOOR))
            hand.put(h, *_chunk_prepare(_silu(q_ref[rows, cols]), 1.0 - f, i_ref[rows, cols], g, g_min))

    def finish(ci):
        rows = _chunk_rows(ci, chunk)
        for h in range(nh):
            cols = slice(h * dk, (h + 1) * dk)
            o, st_ref[h] = _chunk_finish(*hand.get(h), st_ref[h])
            y = _head_norm_gate(o, nw_ref[:, cols], gate_ref[rows, cols])
            y_ref[rows, cols] = y.astype(y_ref.dtype)

    _pipelined_chunks(tb // chunk, prepare, finish)

    @pl.when(t == pl.num_programs(2) - 1)
    def _():
        for h in range(nh):
            sout_ref[h] = st_ref[h].T


def _hgrn(proj, row0, nb, t_len, lb, hg_norm, s0, col0, y_prev):
    _, nh, dk, dv = s0.shape
    w = nh * dk
    hpb = nh if t_len <= LA_CHUNK else _pick(nh, (8, 4, 2, 1))
    bw = hpb * dk
    tb = _pick(t_len, (1024, 512, 256, 128, 64, 32))
    chunk = min(LA_CHUNK, tb)
    nt = t_len // tb
    rb0 = row0 // tb

    def col(g):
        return pl.BlockSpec((tb, bw), lambda b, h, t: (rb0 + b * nt + t, (col0 + g * w) // bw + h))

    vec = pl.BlockSpec((1, bw), lambda b, h, t: (0, h))
    st = pl.BlockSpec((None, hpb, dk, dv), lambda b, h, t: (b, h, 0, 0))
    kern = functools.partial(_hgrn_kernel, chunk=chunk)
    args = [proj, proj, proj, proj, lb.reshape(1, w), hg_norm.reshape(1, w), s0]
    y, s_new = pl.pallas_call(
        kern if y_prev is None else _without_ref(kern, len(args)),
        grid=(nb, nh // hpb, nt),
        in_specs=[col(0), col(1), col(2), col(3), vec, vec, st] + _alias_spec(y_prev),
        out_specs=[pl.BlockSpec((tb, bw), lambda b, h, t: (rb0 + b * nt + t, h)), st],
        out_shape=[jax.ShapeDtypeStruct((proj.shape[0], w), BF16),
                   jax.ShapeDtypeStruct(s0.shape, F32)],
        scratch_shapes=[pltpu.VMEM((hpb, dv, dk), F32)] + _ChunkScratch.shapes(hpb, chunk, dk, dv),
        input_output_aliases=_alias_map(y_prev, len(args)),
        compiler_params=pltpu.CompilerParams(
            dimension_semantics=_ARB3,
            vmem_limit_bytes=_vmem_limit(2 * tb * bw * (4 * 4 + 2) + 5 * hpb * dk * dv * 4)),
        name="hgrn2",
    )(*args, *_alias_arg(y_prev))
    return y, s_new


def _gla_kernel(q_ref, k_ref, v_ref, gate_ref, lr_ref, w2_ref, b2_ref, nw_ref, s0_ref,
                y_ref, sout_ref, st_ref, *scratch, chunk, scale):
    t = pl.program_id(2)
    tb = q_ref.shape[0]
    nh, dk, dv = s0_ref.shape
    hand = _ChunkScratch(scratch)

    @pl.when(t == 0)
    def _():
        for h in range(nh):
            st_ref[h] = s0_ref[h].T

    def prepare(ci):
        rows = _chunk_rows(ci, chunk)
        logits = _dot(lr_ref[rows, :].astype(BF16), w2_ref[...].astype(BF16)) + b2_ref[...]
        g = (jnp.minimum(logits, 0.0) - jnp.log1p(jnp.exp(-jnp.abs(logits)))) / GLA_TAU
        for h in range(nh):
            kc, vc = slice(h * dk, (h + 1) * dk), slice(h * dv, (h + 1) * dv)
            hand.put(h, *_chunk_prepare(q_ref[rows, kc] * scale, k_ref[rows, kc], v_ref[rows, vc],
                                        g[:, kc], None))

    def finish(ci):
        rows = _chunk_rows(ci, chunk)
        for h in range(nh):
            vc = slice(h * dv, (h + 1) * dv)
            o, st_ref[h] = _chunk_finish(*hand.get(h), st_ref[h])
            y_ref[rows, vc] = _head_norm_gate(o, nw_ref[:, vc], gate_ref[rows, vc]).astype(y_ref.dtype)

    _sequential_chunks(tb // chunk, prepare, finish)

    @pl.when(t == pl.num_programs(2) - 1)
    def _():
        for h in range(nh):
            sout_ref[h] = st_ref[h].T


def _gla(proj, lr, row0, nb, t_len, w2, b2, gla_norm, s0, col_q, col_k, col_v, col_g, y_prev):
    _, nh, dk, dv = s0.shape
    rank = lr.shape[1]
    hpb = nh if t_len <= LA_CHUNK else _pick(nh, (4, 2, 1))
    kw, vw = hpb * dk, hpb * dv
    tb = _pick(t_len, (512, 256, 128, 64, 32))
    chunk = min(LA_CHUNK, tb)
    nt = t_len // tb
    rb0 = row0 // tb

    def col(c0, wd):
        return pl.BlockSpec((tb, wd), lambda b, h, t: (rb0 + b * nt + t, c0 // wd + h))

    st = pl.BlockSpec((None, hpb, dk, dv), lambda b, h, t: (b, h, 0, 0))
    kern = functools.partial(_gla_kernel, chunk=chunk, scale=float(dk) ** -0.5)
    args = [proj, proj, proj, proj, lr, w2, b2.reshape(1, nh * dk),
            gla_norm.reshape(1, nh * dv), s0]
    y, s_new = pl.pallas_call(
        kern if y_prev is None else _without_ref(kern, len(args)),
        grid=(nb, nh // hpb, nt),
        in_specs=[col(col_q, kw), col(col_k, kw), col(col_v, vw), col(col_g, vw),
                  pl.BlockSpec((tb, rank), lambda b, h, t: (rb0 + b * nt + t, 0)),
                  pl.BlockSpec((rank, kw), lambda b, h, t: (0, h)),
                  pl.BlockSpec((1, kw), lambda b, h, t: (0, h)),
                  pl.BlockSpec((1, vw), lambda b, h, t: (0, h)),
                  st] + _alias_spec(y_prev),
        out_specs=[pl.BlockSpec((tb, vw), lambda b, h, t: (rb0 + b * nt + t, h)), st],
        out_shape=[jax.ShapeDtypeStruct((proj.shape[0], nh * dv), BF16),
                   jax.ShapeDtypeStruct(s0.shape, F32)],
        scratch_shapes=[pltpu.VMEM((hpb, dv, dk), F32)] + _ChunkScratch.shapes(hpb, chunk, dk, dv),
        input_output_aliases=_alias_map(y_prev, len(args)),
        compiler_params=pltpu.CompilerParams(
            dimension_semantics=_ARB3,
            vmem_limit_bytes=_vmem_limit(2 * tb * (2 * kw * 4 + 2 * vw * 4 + vw * 2) + 5 * hpb * dk * dv * 4)),
        name="gla",
    )(*args, *_alias_arg(y_prev))
    return y, s_new


def _lru_kernel(x_ref, gate_ref, cs_ref, h0_ref, cw_ref, cb_ref, wa_ref, ba_ref, wx_ref, bx_ref,
                lam_ref, y_ref, hout_ref, cout_ref, hist_ref, hcar_ref, a_ref, u_ref, h_ref):
    t = pl.program_id(2)
    tb, wd = x_ref.shape
    ncw = cw_ref.shape[0]
    nblk, bd, _ = wa_ref.shape

    @pl.when(t == 0)
    def _():
        hist_ref[...] = jnp.zeros_like(hist_ref)
        hist_ref[SUBLANES - (ncw - 1):SUBLANES, :] = cs_ref[...]
        hcar_ref[...] = jnp.broadcast_to(h0_ref[...], (SUBLANES, wd))

    x = x_ref[...]
    xe = jnp.concatenate([hist_ref[...], x], axis=0)
    cw = cw_ref[...]
    xc = cb_ref[...] + x * cw[ncw - 1:ncw, :]
    for s in range(1, ncw):
        xc = xc + pltpu.roll(xe, s, 0)[SUBLANES:, :] * cw[ncw - 1 - s:ncw - s, :]
    hist_ref[...] = x[tb - SUBLANES:tb, :]

    xb = xc.astype(BF16)
    ra, ri = [], []
    for j in range(nblk):
        xj = xb[:, j * bd:(j + 1) * bd]
        ra.append(_dot(xj, wa_ref[j].astype(BF16)))
        ri.append(_dot(xj, wx_ref[j].astype(BF16)))
    r = _sigmoid(jnp.concatenate(ra, axis=1) + ba_ref[...])
    gi = _sigmoid(jnp.concatenate(ri, axis=1) + bx_ref[...])
    lam = lam_ref[...]
    softplus_neg = jnp.maximum(-lam, 0.0) + jnp.log1p(jnp.exp(-jnp.abs(lam)))
    log_a = -LRU_C * r * softplus_neg
    a = jnp.exp(log_a)
    x2 = 2.0 * log_a
    e2 = jnp.exp(x2)
    d2 = 1.0 - e2
    small = jnp.where(e2 == 1.0, -x2, d2 * x2 / jnp.log(e2))
    one_m_a2 = jnp.where(x2 < -0.5, d2, small)
    root = jnp.where(one_m_a2 > 0.0, one_m_a2 * lax.rsqrt(one_m_a2), 0.0)
    u = root * (gi * xc)

    rowi = _tile_row_index(tb, wd)
    s = 1
    while s < SUBLANES:
        ok = rowi >= s
        u = u + jnp.where(ok, a * _tile_roll(u, s), 0.0)
        a = a * jnp.where(ok, _tile_roll(a, s), 1.0)
        s *= 2
    a_ref[...] = a
    u_ref[...] = u

    def body(i, hp):
        rows = pl.ds(pl.multiple_of(i * SUBLANES, SUBLANES), SUBLANES)
        h = u_ref[rows, :] + a_ref[rows, :] * hp
        h_ref[rows, :] = h
        return jnp.broadcast_to(h[SUBLANES - 1:SUBLANES, :], (SUBLANES, wd))

    hp = lax.fori_loop(0, tb // SUBLANES, body, hcar_ref[...], unroll=4)
    hcar_ref[...] = hp
    y_ref[...] = (h_ref[...] * _silu(gate_ref[...])).astype(y_ref.dtype)

    @pl.when(t == pl.num_programs(2) - 1)
    def _():
        hout_ref[...] = hp[0:1, :]
        cout_ref[...] = x[tb - (ncw - 1):tb, :]


def _lru(proj, row0, nb, t_len, conv_state, h0, conv_w, conv_b, wa, ba, wx, bx, lam, col_x, col_g, y_prev):
    nblk, bd, _ = wa.shape
    w = nblk * bd
    ncw = conv_w.shape[0]
    bps = nblk if t_len <= LA_CHUNK else _pick(nblk, (4, 2, 1))
    bw = bps * bd
    tb = _pick(t_len, (512, 256, 128, 64, 32))
    nt = t_len // tb
    rb0 = row0 // tb

    def col(c0):
        return pl.BlockSpec((tb, bw), lambda b, h, t: (rb0 + b * nt + t, c0 // bw + h))

    vec = pl.BlockSpec((1, bw), lambda b, h, t: (0, h))
    blk = pl.BlockSpec((bps, bd, bd), lambda b, h, t: (h, 0, 0))
    hspec = pl.BlockSpec((None, 1, bw), lambda b, h, t: (b, 0, h))
    cspec = pl.BlockSpec((None, ncw - 1, bw), lambda b, h, t: (b, 0, h))
    args = [proj, proj, conv_state, h0.reshape(nb, 1, w), conv_w, conv_b.reshape(1, w),
            wa, ba.reshape(1, w), wx, bx.reshape(1, w), lam.reshape(1, w)]
    y, h_new, c_new = pl.pallas_call(
        _lru_kernel if y_prev is None else _without_ref(_lru_kernel, len(args)),
        grid=(nb, nblk // bps, nt),
        in_specs=[col(col_x), col(col_g), cspec, hspec,
                  pl.BlockSpec((ncw, bw), lambda b, h, t: (0, h)), vec,
                  blk, vec, blk, vec, vec] + _alias_spec(y_prev),
        out_specs=[pl.BlockSpec((tb, bw), lambda b, h, t: (rb0 + b * nt + t, h)), hspec, cspec],
        out_shape=[jax.ShapeDtypeStruct((proj.shape[0], w), BF16),
                   jax.ShapeDtypeStruct((nb, 1, w), F32),
                   jax.ShapeDtypeStruct((nb, ncw - 1, w), F32)],
        scratch_shapes=[pltpu.VMEM((SUBLANES, bw), F32), pltpu.VMEM((SUBLANES, bw), F32),
                        pltpu.VMEM((tb, bw), F32), pltpu.VMEM((tb, bw), F32),
                        pltpu.VMEM((tb, bw), F32)],
        input_output_aliases=_alias_map(y_prev, len(args)),
        compiler_params=pltpu.CompilerParams(dimension_semantics=_ARB3),
        name="rglru",
    )(*args, *_alias_arg(y_prev))
    return y, h_new.reshape(nb, w), c_new


def _merge_kernel(ya_ref, yb_ref, yc_ref, w_ref, m0_ref, m1_ref, m2_ref, o_ref):
    acc = _sigmoid(m0_ref[...]) * _dot(ya_ref[...], w_ref[0].astype(BF16))
    acc = acc + _sigmoid(m1_ref[...]) * _dot(yb_ref[...], w_ref[1].astype(BF16))
    acc = acc + _sigmoid(m2_ref[...]) * _dot(yc_ref[...], w_ref[2].astype(BF16))
    o_ref[...] = acc.astype(o_ref.dtype)


def _merge(ya, yb, yc, wbr, layer, proj, col_m):
    m, bw = ya.shape
    _, nbr, _, d = wbr.shape
    tm = _pick(m, (1408, 1056, 1024, 512, 320, 256, 128, 64, 32, 16))
    tn = next(c for c in (256, 128) if d % c == 0 and col_m % c == 0)
    yspec = pl.BlockSpec((tm, bw), lambda i, j: (i, 0), pipeline_mode=pl.Buffered(1))

    def mg(n):
        return pl.BlockSpec((tm, tn), lambda i, j: (i, (col_m + n * d) // tn + j))

    need = (3 * tm * bw * 2 + 2 * nbr * bw * tn * 4 + nbr * bw * tn * 2
            + 2 * 3 * tm * tn * 4 + 6 * tm * tn * 4)
    return pl.pallas_call(
        _merge_kernel,
        grid=(m // tm, d // tn),
        in_specs=[yspec, yspec, yspec,
                  pl.BlockSpec((None, nbr, bw, tn), lambda i, j: (layer, 0, 0, j)),
                  mg(0), mg(1), mg(2)],
        out_specs=pl.BlockSpec((tm, tn), lambda i, j: (i, j)),
        out_shape=jax.ShapeDtypeStruct((m, d), BF16),
        compiler_params=pltpu.CompilerParams(
            dimension_semantics=_ARB2, vmem_limit_bytes=_vmem_limit(need)),
        name="merge",
    )(ya, yb, yc, wbr, proj, proj, proj)


def kernel(x_prompt, x_sample, state_hgrn, state_lru_h, state_lru_conv, state_gla, norm_pre, norm_post, w_in, hg_lb_logits, hg_norm, lru_conv_w, lru_conv_b, lru_wa, lru_ba, lru_wx, lru_bx, lru_lambda, gla_w2, gla_b2, gla_norm, w_branch, w_out):
    bp, tp, d = x_prompt.shape
    bs, ts, _ = x_sample.shape
    depth, _, in_cols = w_in.shape
    _, _, hg_h, hg_dk, hg_dv = state_hgrn.shape
    hg_w = hg_h * hg_dk
    lru_w = state_lru_h.shape[-1]
    _, _, gl_h, gl_dk, gl_dv = state_gla.shape
    gl_kw, gl_vw = gl_h * gl_dk, gl_h * gl_dv
    rank = gla_w2.shape[1]
    dt = x_prompt.dtype

    c_hg = 0
    c_lx = 4 * hg_w
    c_lg = c_lx + lru_w
    c_q = c_lg + lru_w
    c_k = c_q + gl_kw
    c_v = c_k + gl_kw
    c_lr = c_v + gl_vw
    c_cg = c_lr
    c_m = c_cg + gl_vw
    n_main = in_cols - rank
    assert c_m + 3 * d == n_main

    lb_sm = jax.nn.softmax(hg_lb_logits.astype(F32), axis=0)
    lb_all = jnp.cumsum(lb_sm, axis=0) - lb_sm[0:1]
    w_in_t = jnp.transpose(w_in, (0, 2, 1))

    mp, ms = bp * tp, bs * ts
    xs = (x_prompt.reshape(mp, d), x_sample.reshape(ms, d))

    zeros = lambda shape: jnp.zeros(shape, dt)
    outs = {k: [] for k in ("hg_p", "hg_s", "lh_p", "lh_s", "lc_p", "lc_s", "gl_p", "gl_s")}
    z = _prenorm(xs, norm_pre[0])
    for l in range(depth):
        proj, lr = _matmul_t(z, w_in_t, l, c_lr, rank, "in_proj")

        groups = (
            ("p", 0, bp, tp, zeros((bp,) + state_hgrn.shape[2:]), zeros((bp, lru_w)),
             zeros((bp,) + state_lru_conv.shape[2:]), zeros((bp,) + state_gla.shape[2:])),
            ("s", mp, bs, ts, state_hgrn[l], state_lru_h[l], state_lru_conv[l], state_gla[l]),
        )
        ya = jnp.zeros((mp + ms, hg_w), BF16)
        yb = jnp.zeros((mp + ms, lru_w), BF16)
        yc = jnp.zeros((mp + ms, gl_vw), BF16)
        for tag, row0, nb, t_len, s_hg, s_lh, s_lc, s_gl in groups:
            ya, s = _hgrn(proj, row0, nb, t_len, lb_all[l], hg_norm[l], s_hg, c_hg, ya)
            outs["hg_" + tag].append(s)
            yb, hn, cn = _lru(proj, row0, nb, t_len, s_lc, s_lh, lru_conv_w[l], lru_conv_b[l],
                              lru_wa[l], lru_ba[l], lru_wx[l], lru_bx[l], lru_lambda[l], c_lx, c_lg, yb)
            outs["lh_" + tag].append(hn); outs["lc_" + tag].append(cn)
            yc, s = _gla(proj, lr, row0, nb, t_len, gla_w2[l], gla_b2[l], gla_norm[l], s_gl,
                         c_q, c_k, c_v, c_cg, yc)
            outs["gl_" + tag].append(s)

        merged = _merge(ya, yb, yc, w_branch, l, proj, c_m)
        out = _matmul(merged, w_out, l, "out_proj")
        if l < depth - 1:
            x, z = _postnorm(xs, out, norm_post[l], mp, ms, norm_pre[l + 1])
            xs = (x,)
        else:
            xs = _postnorm(xs, out, norm_post[l], mp, ms, None)

    yp = xs[0].reshape(bp, tp, d)
    ys = xs[1].reshape(bs, ts, d)
    st = lambda k: jnp.stack(outs[k])
    return (yp, ys, st("hg_p"), st("hg_s"), st("lh_p"), st("lh_s"),
            st("lc_p"), st("lc_s"), st("gl_p"), st("gl_s"))
```

```python
import functools
import math

import jax
import jax.numpy as jnp
from jax import lax
from jax.experimental import pallas as pl
from jax.experimental.pallas import tpu as pltpu

F32 = jnp.float32
BF16 = jnp.bfloat16

EPS = 1e-6
F_FLOOR = 1e-6
LRU_C = 8.0
GLA_TAU = 16.0
SUBLANES = 8
LANES = 128
LA_CHUNK = 64
V7X_VMEM_LIMIT = 60000 * 1024


def _vmem_limit(nbytes):
    return int(min(V7X_VMEM_LIMIT, max(32 * 1024 * 1024, nbytes * 5 // 4 + (4 << 20))))


def _pick(n, candidates):
    for c in candidates:
        if n % c == 0:
            return c
    return n


def _dot(a, b):
    return jnp.dot(a, b, preferred_element_type=F32)


def _dot_nt(a, b):
    return lax.dot_general(a, b, (((1,), (1,)), ((), ())), preferred_element_type=F32)


def _dot_tn(a, b):
    return lax.dot_general(a, b, (((0,), (0,)), ((), ())), preferred_element_type=F32)


def _sigmoid(x):
    return jax.nn.sigmoid(x)


def _silu(x):
    return x * jax.nn.sigmoid(x)


def _without_ref(kernel, idx):
    def wrapped(*refs):
        return kernel(*refs[:idx], *refs[idx + 1:])
    return wrapped


def _alias_spec(y_prev):
    return [] if y_prev is None else [pl.BlockSpec(memory_space=pl.ANY)]


def _alias_arg(y_prev):
    return [] if y_prev is None else [y_prev]


def _alias_map(y_prev, idx):
    return {} if y_prev is None else {idx: 0}


_ARB2 = ("arbitrary", "arbitrary")
_ARB3 = ("arbitrary", "arbitrary", "arbitrary")


def _rms(x, w):
    return x * lax.rsqrt(jnp.mean(x * x, axis=-1, keepdims=True) + EPS) * w


def _group_specs(mp, ms, d):
    tr = next(c for c in (256, 128, 64, 32, 16, 8) if mp % c == 0 and ms % c == 0)
    nbp = mp // tr
    pspec = pl.BlockSpec((tr, d), lambda i: (jnp.minimum(i, nbp - 1), 0))
    sspec = pl.BlockSpec((tr, d), lambda i: (jnp.maximum(i - nbp, 0), 0))
    return tr, nbp, pspec, sspec


def _prenorm2_kernel(xp_ref, xs_ref, w_ref, z_ref, *, nbp):
    x = jnp.where(pl.program_id(0) < nbp, xp_ref[...], xs_ref[...])
    z_ref[...] = _rms(x, w_ref[...]).astype(z_ref.dtype)


def _prenorm(xs, w):
    d = w.shape[0]
    mp, ms = xs[0].shape[0], xs[1].shape[0]
    tr, nbp, pspec, sspec = _group_specs(mp, ms, d)
    return pl.pallas_call(
        functools.partial(_prenorm2_kernel, nbp=nbp),
        grid=((mp + ms) // tr,),
        in_specs=[pspec, sspec, pl.BlockSpec((1, d), lambda i: (0, 0))],
        out_specs=pl.BlockSpec((tr, d), lambda i: (i, 0)),
        out_shape=jax.ShapeDtypeStruct((mp + ms, d), BF16),
        compiler_params=pltpu.CompilerParams(
            dimension_semantics=("arbitrary",), vmem_limit_bytes=_vmem_limit(2 * 3 * tr * d * 4)),
        name="prenorm",
    )(xs[0], xs[1], w.reshape(1, d))


def _postnorm_kernel(*refs, nbp, n_in, last):
    i = pl.program_id(0)
    o_ref, w_ref = refs[n_in], refs[n_in + 1]
    x = refs[0][...] if n_in == 1 else jnp.where(i < nbp, refs[0][...], refs[1][...])
    y = x + _rms(o_ref[...], w_ref[...])
    if last:
        yp_ref, ys_ref = refs[n_in + 2:]

        @pl.when(i < nbp)
        def _():
            yp_ref[...] = y

        @pl.when(i >= nbp)
        def _():
            ys_ref[...] = y
    else:
        wn_ref, y_ref, z_ref = refs[n_in + 2:]
        y_ref[...] = y
        z_ref[...] = _rms(y, wn_ref[...]).astype(z_ref.dtype)


def _postnorm(xs, out, w, mp, ms, w_next):
    d = w.shape[0]
    tr, nbp, pspec, sspec = _group_specs(mp, ms, d)
    full = pl.BlockSpec((tr, d), lambda i: (i, 0))
    vec = pl.BlockSpec((1, d), lambda i: (0, 0))
    in_specs = ([pspec, sspec] if len(xs) == 2 else [full]) + [full, vec]
    args = [*xs, out, w.reshape(1, d)]
    if w_next is None:
        out_specs = [pspec, sspec]
        out_shape = [jax.ShapeDtypeStruct((mp, d), F32), jax.ShapeDtypeStruct((ms, d), F32)]
    else:
        in_specs.append(vec)
        args.append(w_next.reshape(1, d))
        out_specs = [full, full]
        out_shape = [jax.ShapeDtypeStruct((mp + ms, d), F32), jax.ShapeDtypeStruct((mp + ms, d), BF16)]
    return pl.pallas_call(
        functools.partial(_postnorm_kernel, nbp=nbp, n_in=len(xs), last=w_next is None),
        grid=((mp + ms) // tr,),
        in_specs=in_specs, out_specs=out_specs, out_shape=out_shape,
        compiler_params=pltpu.CompilerParams(
            dimension_semantics=("arbitrary",),
            vmem_limit_bytes=_vmem_limit(2 * (len(in_specs) + len(out_specs)) * tr * d * 4)),
        name="postnorm",
    )(*args)


def _mm_kernel(a_ref, w_ref, o_ref):
    o_ref[...] = _dot(a_ref[...], w_ref[...].astype(BF16))


def _mm_t_kernel(a_ref, wt_ref, ws_ref, o_ref, os_ref):
    o_ref[...] = _dot_nt(a_ref[...], wt_ref[0].astype(BF16))

    @pl.when(pl.program_id(1) == 0)
    def _():
        os_ref[...] = _dot_nt(a_ref[...], ws_ref[0].astype(BF16))


BIG_TM = (2112, 1408, 1056, 1024, 512, 320, 256, 128, 64, 32, 16)


def _matmul(a, w, layer, name):
    m, k = a.shape
    n = w.shape[2]
    tm = _pick(m, BIG_TM)
    tn = _pick(n, (512, 256, 128))
    need = tm * k * 2 + 2 * k * tn * 4 + k * tn * 2 + 3 * tm * tn * 4
    return pl.pallas_call(
        _mm_kernel,
        grid=(m // tm, n // tn),
        in_specs=[pl.BlockSpec((tm, k), lambda i, j: (i, 0), pipeline_mode=pl.Buffered(1)),
                  pl.BlockSpec((None, k, tn), lambda i, j: (layer, 0, j))],
        out_specs=pl.BlockSpec((tm, tn), lambda i, j: (i, j)),
        out_shape=jax.ShapeDtypeStruct((m, n), F32),
        compiler_params=pltpu.CompilerParams(
            dimension_semantics=_ARB2, vmem_limit_bytes=_vmem_limit(need)),
        name=name,
    )(a, w)


def _matmul_t(a, wt, layer, skip_start, skip, name):
    m, k = a.shape
    n_out = wt.shape[1] - skip
    tm = _pick(m, BIG_TM)
    tn = next(c for c in (512, 256, 128) if n_out % c == 0 and skip_start % c == 0)
    assert skip % SUBLANES == 0

    def w_index(i, j):
        row = j * tn + jnp.where(j * tn >= skip_start, skip, 0)
        return layer, pl.multiple_of(row, SUBLANES), 0

    need = tm * k * 2 + 2 * k * tn * 4 + k * tn * 2 + 3 * tm * tn * 4 + 2 * tm * LANES * 4
    return pl.pallas_call(
        _mm_t_kernel,
        grid=(m // tm, n_out // tn),
        in_specs=[pl.BlockSpec((tm, k), lambda i, j: (i, 0), pipeline_mode=pl.Buffered(1)),
                  pl.BlockSpec((pl.Element(1), pl.Element(tn), pl.Element(k)), w_index),
                  pl.BlockSpec((pl.Element(1), pl.Element(skip), pl.Element(k)),
                               lambda i, j: (layer, skip_start, 0))],
        out_specs=[pl.BlockSpec((tm, tn), lambda i, j: (i, j)),
                   pl.BlockSpec((tm, skip), lambda i, j: (i, 0))],
        out_shape=[jax.ShapeDtypeStruct((m, n_out), F32), jax.ShapeDtypeStruct((m, skip), F32)],
        compiler_params=pltpu.CompilerParams(
            dimension_semantics=_ARB2, vmem_limit_bytes=_vmem_limit(need)),
        name=name,
    )(a, wt, wt)


def _tile_row_index(n, d):
    return lax.broadcasted_iota(jnp.int32, (n, d), 0) & (SUBLANES - 1)


def _tile_roll(x, s):
    n, d = x.shape
    return pltpu.roll(x.reshape(n // SUBLANES, SUBLANES, d), s, 1).reshape(n, d)


def _tile_cumsum(g, rowi):
    c = g
    s = 1
    while s < SUBLANES:
        c = c + jnp.where(rowi >= s, _tile_roll(c, s), 0.0)
        s *= 2
    return c


def _tile_row_bcast(x, r):
    n, d = x.shape
    return jnp.concatenate(
        [jnp.broadcast_to(x[SUBLANES * i + r:SUBLANES * i + r + 1, :], (SUBLANES, d))
         for i in range(n // SUBLANES)], axis=0)


def _chunk_prepare(q, k, v, g, g_min):
    L, dk = q.shape
    nt = L // SUBLANES
    rowk = _tile_row_index(L, dk)

    c = _tile_cumsum(g, rowk)
    tot = [c[SUBLANES * i + SUBLANES - 1:SUBLANES * (i + 1), :] for i in range(nt)]
    qt = q * jnp.exp(c)
    knew = k * jnp.exp(_tile_row_bcast(c, SUBLANES - 1) - c)

    row = lax.broadcasted_iota(jnp.int32, (L, L), 0)
    col = lax.broadcasted_iota(jnp.int32, (L, L), 1)
    offs = jnp.where((row >> 3) == (col >> 3), row - col, -1)
    half = SUBLANES // 2
    if g_min is not None and -g_min * half < 80.0:
        cm = c - _tile_row_bcast(c, half - 1)
        inner = _dot_nt((q * jnp.exp(cm)).astype(BF16), (k * jnp.exp(-cm)).astype(BF16))
        band = jnp.where(offs >= 0, inner, 0.0)
    else:
        eg = jnp.exp(g)
        w = k
        band = jnp.where(offs == 0, jnp.sum(q * w, axis=1, keepdims=True), 0.0)
        for d in range(1, SUBLANES):
            w = _tile_roll(w, 1) * eg
            band = jnp.where(offs == d, jnp.sum(q * w, axis=1, keepdims=True), band)

    tiles = []
    arows = [jnp.zeros((SUBLANES, L), F32)]
    qhat = [qt[0:SUBLANES]]
    er = None
    for i in range(nt):
        lo, hi = SUBLANES * i, SUBLANES * (i + 1)
        if i > 0:
            pad = jnp.zeros((L - lo, dk), F32)
            kh = jnp.concatenate(tiles + [pad], axis=0).astype(BF16)
            arows.append(_dot_nt(qt[lo:hi].astype(BF16), kh))
            qhat.append(qt[lo:hi] * er)
        dcy = jnp.exp(tot[i])
        tiles = [t * dcy for t in tiles] + [knew[lo:hi]]
        er = dcy if er is None else er * dcy

    khat = jnp.concatenate(tiles, axis=0).astype(BF16)
    qh = jnp.concatenate(qhat, axis=0).astype(BF16)
    scores = (jnp.concatenate(arows, axis=0) + band).astype(BF16)
    return scores, v.astype(BF16), qh, khat, er


def _chunk_finish(scores, vb, qh, khat, er, st):
    o = _dot(scores, vb) + _dot_nt(qh, st.astype(BF16))
    return o, st * er + _dot_tn(vb, khat)


class _ChunkScratch:
    def __init__(self, refs):
        self.sc, self.vb, self.qh, self.kh, self.er = refs

    @staticmethod
    def shapes(nh, chunk, dk, dv):
        return [pltpu.VMEM((nh, chunk, chunk), BF16), pltpu.VMEM((nh, chunk, dv), BF16),
                pltpu.VMEM((nh, chunk, dk), BF16), pltpu.VMEM((nh, chunk, dk), BF16),
                pltpu.VMEM((nh, SUBLANES, dk), F32)]

    def put(self, h, scores, vb, qh, khat, er):
        self.sc[h] = scores
        self.vb[h] = vb
        self.qh[h] = qh
        self.kh[h] = khat
        self.er[h] = jnp.broadcast_to(er, self.er.shape[1:])

    def get(self, h):
        return self.sc[h], self.vb[h], self.qh[h], self.kh[h], self.er[h][0:1, :]


def _pipelined_chunks(n, prepare, finish):
    prepare(0)

    def body(ci, carry):
        finish(ci - 1)
        prepare(ci)
        return carry

    lax.fori_loop(1, n, body, 0)
    finish(n - 1)


def _sequential_chunks(n, prepare, finish):
    def body(ci, carry):
        prepare(ci)
        finish(ci)
        return carry

    lax.fori_loop(0, n, body, 0)


def _head_norm_gate(o, w, gate):
    on = o * lax.rsqrt(jnp.mean(o * o, axis=-1, keepdims=True) + EPS)
    return on * w * _silu(gate)


def _chunk_rows(ci, chunk):
    start = ci * chunk
    return pl.ds(start if isinstance(start, int) else pl.multiple_of(start, chunk), chunk)


def _hgrn_kernel(q_ref, f_ref, i_ref, gate_ref, lb_ref, nw_ref, s0_ref,
                 y_ref, sout_ref, st_ref, *scratch, chunk):
    t = pl.program_id(2)
    tb = q_ref.shape[0]
    nh, dk, dv = s0_ref.shape
    hand = _ChunkScratch(scratch)
    g_min = math.log(F_FLOOR) - 1e-3

    @pl.when(t == 0)
    def _():
        for h in range(nh):
            st_ref[h] = s0_ref[h].T

    def prepare(ci):
        rows = _chunk_rows(ci, chunk)
        for h in range(nh):
            cols = slice(h * dk, (h + 1) * dk)
            lb = lb_ref[:, cols]
            f = lb + (1.0 - lb) * _sigmoid(f_ref[rows, cols])
            g = jnp.log(jnp.maximum(f, F_FLOOR))
            hand.put(h, *_chunk_prepare(_silu(q_ref[rows, cols]), 1.0 - f, i_ref[rows, cols], g, g_min))

    def finish(ci):
        rows = _chunk_rows(ci, chunk)
        for h in range(nh):
            cols = slice(h * dk, (h + 1) * dk)
            o, st_ref[h] = _chunk_finish(*hand.get(h), st_ref[h])
            y = _head_norm_gate(o, nw_ref[:, cols], gate_ref[rows, cols])
            y_ref[rows, cols] = y.astype(y_ref.dtype)

    _pipelined_chunks(tb // chunk, prepare, finish)

    @pl.when(t == pl.num_programs(2) - 1)
    def _():
        for h in range(nh):
            sout_ref[h] = st_ref[h].T


def _hgrn(proj, row0, nb, t_len, lb, hg_norm, s0, col0, y_prev):
    _, nh, dk, dv = s0.shape
    w = nh * dk
    hpb = nh if t_len <= LA_CHUNK else _pick(nh, (8, 4, 2, 1))
    bw = hpb * dk
    tb = _pick(t_len, (1024, 512, 256, 128, 64, 32))
    chunk = min(LA_CHUNK, tb)
    nt = t_len // tb
    rb0 = row0 // tb

    def col(g):
        return pl.BlockSpec((tb, bw), lambda b, h, t: (rb0 + b * nt + t, (col0 + g * w) // bw + h))

    vec = pl.BlockSpec((1, bw), lambda b, h, t: (0, h))
    st = pl.BlockSpec((None, hpb, dk, dv), lambda b, h, t: (b, h, 0, 0))
    kern = functools.partial(_hgrn_kernel, chunk=chunk)
    args = [proj, proj, proj, proj, lb.reshape(1, w), hg_norm.reshape(1, w), s0]
    y, s_new = pl.pallas_call(
        kern if y_prev is None else _without_ref(kern, len(args)),
        grid=(nb, nh // hpb, nt),
        in_specs=[col(0), col(1), col(2), col(3), vec, vec, st] + _alias_spec(y_prev),
        out_specs=[pl.BlockSpec((tb, bw), lambda b, h, t: (rb0 + b * nt + t, h)), st],
        out_shape=[jax.ShapeDtypeStruct((proj.shape[0], w), BF16),
                   jax.ShapeDtypeStruct(s0.shape, F32)],
        scratch_shapes=[pltpu.VMEM((hpb, dv, dk), F32)] + _ChunkScratch.shapes(hpb, chunk, dk, dv),
        input_output_aliases=_alias_map(y_prev, len(args)),
        compiler_params=pltpu.CompilerParams(
            dimension_semantics=_ARB3,
            vmem_limit_bytes=_vmem_limit(2 * tb * bw * (4 * 4 + 2) + 5 * hpb * dk * dv * 4)),
        name="hgrn2",
    )(*args, *_alias_arg(y_prev))
    return y, s_new


def _gla_kernel(q_ref, k_ref, v_ref, gate_ref, lr_ref, w2_ref, b2_ref, nw_ref, s0_ref,
                y_ref, sout_ref, st_ref, *scratch, chunk, scale):
    t = pl.program_id(2)
    tb = q_ref.shape[0]
    nh, dk, dv = s0_ref.shape
    hand = _ChunkScratch(scratch)

    @pl.when(t == 0)
    def _():
        for h in range(nh):
            st_ref[h] = s0_ref[h].T

    def prepare(ci):
        rows = _chunk_rows(ci, chunk)
        logits = _dot(lr_ref[rows, :].astype(BF16), w2_ref[...].astype(BF16)) + b2_ref[...]
        g = (jnp.minimum(logits, 0.0) - jnp.log1p(jnp.exp(-jnp.abs(logits)))) / GLA_TAU
        for h in range(nh):
            kc, vc = slice(h * dk, (h + 1) * dk), slice(h * dv, (h + 1) * dv)
            hand.put(h, *_chunk_prepare(q_ref[rows, kc] * scale, k_ref[rows, kc], v_ref[rows, vc],
                                        g[:, kc], None))

    def finish(ci):
        rows = _chunk_rows(ci, chunk)
        for h in range(nh):
            vc = slice(h * dv, (h + 1) * dv)
            o, st_ref[h] = _chunk_finish(*hand.get(h), st_ref[h])
            y_ref[rows, vc] = _head_norm_gate(o, nw_ref[:, vc], gate_ref[rows, vc]).astype(y_ref.dtype)

    _sequential_chunks(tb // chunk, prepare, finish)

    @pl.when(t == pl.num_programs(2) - 1)
    def _():
        for h in range(nh):
            sout_ref[h] = st_ref[h].T


def _gla(proj, lr, row0, nb, t_len, w2, b2, gla_norm, s0, col_q, col_k, col_v, col_g, y_prev):
    _, nh, dk, dv = s0.shape
    rank = lr.shape[1]
    hpb = nh if t_len <= LA_CHUNK else _pick(nh, (4, 2, 1))
    kw, vw = hpb * dk, hpb * dv
    tb = _pick(t_len, (512, 256, 128, 64, 32))
    chunk = min(LA_CHUNK, tb)
    nt = t_len // tb
    rb0 = row0 // tb

    def col(c0, wd):
        return pl.BlockSpec((tb, wd), lambda b, h, t: (rb0 + b * nt + t, c0 // wd + h))

    st = pl.BlockSpec((None, hpb, dk, dv), lambda b, h, t: (b, h, 0, 0))
    kern = functools.partial(_gla_kernel, chunk=chunk, scale=float(dk) ** -0.5)
    args = [proj, proj, proj, proj, lr, w2, b2.reshape(1, nh * dk),
            gla_norm.reshape(1, nh * dv), s0]
    y, s_new = pl.pallas_call(
        kern if y_prev is None else _without_ref(kern, len(args)),
        grid=(nb, nh // hpb, nt),
        in_specs=[col(col_q, kw), col(col_k, kw), col(col_v, vw), col(col_g, vw),
                  pl.BlockSpec((tb, rank), lambda b, h, t: (rb0 + b * nt + t, 0)),
                  pl.BlockSpec((rank, kw), lambda b, h, t: (0, h)),
                  pl.BlockSpec((1, kw), lambda b, h, t: (0, h)),
                  pl.BlockSpec((1, vw), lambda b, h, t: (0, h)),
                  st] + _alias_spec(y_prev),
        out_specs=[pl.BlockSpec((tb, vw), lambda b, h, t: (rb0 + b * nt + t, h)), st],
        out_shape=[jax.ShapeDtypeStruct((proj.shape[0], nh * dv), BF16),
                   jax.ShapeDtypeStruct(s0.shape, F32)],
        scratch_shapes=[pltpu.VMEM((hpb, dv, dk), F32)] + _ChunkScratch.shapes(hpb, chunk, dk, dv),
        input_output_aliases=_alias_map(y_prev, len(args)),
        compiler_params=pltpu.CompilerParams(
            dimension_semantics=_ARB3,
            vmem_limit_bytes=_vmem_limit(2 * tb * (2 * kw * 4 + 2 * vw * 4 + vw * 2) + 5 * hpb * dk * dv * 4)),
        name="gla",
    )(*args, *_alias_arg(y_prev))
    return y, s_new


def _lru_kernel(x_ref, gate_ref, cs_ref, h0_ref, cw_ref, cb_ref, wa_ref, ba_ref, wx_ref, bx_ref,
                lam_ref, y_ref, hout_ref, cout_ref, hist_ref, hcar_ref, a_ref, u_ref, h_ref):
    t = pl.program_id(2)
    tb, wd = x_ref.shape
    ncw = cw_ref.shape[0]
    nblk, bd, _ = wa_ref.shape

    @pl.when(t == 0)
    def _():
        hist_ref[...] = jnp.zeros_like(hist_ref)
        hist_ref[SUBLANES - (ncw - 1):SUBLANES, :] = cs_ref[...]
        hcar_ref[...] = jnp.broadcast_to(h0_ref[...], (SUBLANES, wd))

    x = x_ref[...]
    xe = jnp.concatenate([hist_ref[...], x], axis=0)
    cw = cw_ref[...]
    xc = cb_ref[...] + x * cw[ncw - 1:ncw, :]
    for s in range(1, ncw):
        xc = xc + pltpu.roll(xe, s, 0)[SUBLANES:, :] * cw[ncw - 1 - s:ncw - s, :]
    hist_ref[...] = x[tb - SUBLANES:tb, :]

    xb = xc.astype(BF16)
    ra, ri = [], []
    for j in range(nblk):
        xj = xb[:, j * bd:(j + 1) * bd]
        ra.append(_dot(xj, wa_ref[j].astype(BF16)))
        ri.append(_dot(xj, wx_ref[j].astype(BF16)))
    r = _sigmoid(jnp.concatenate(ra, axis=1) + ba_ref[...])
    gi = _sigmoid(jnp.concatenate(ri, axis=1) + bx_ref[...])
    lam = lam_ref[...]
    softplus_neg = jnp.maximum(-lam, 0.0) + jnp.log1p(jnp.exp(-jnp.abs(lam)))
    log_a = -LRU_C * r * softplus_neg
    a = jnp.exp(log_a)
    x2 = 2.0 * log_a
    e2 = jnp.exp(x2)
    d2 = 1.0 - e2
    small = jnp.where(e2 == 1.0, -x2, d2 * x2 / jnp.log(e2))
    one_m_a2 = jnp.where(x2 < -0.5, d2, small)
    root = jnp.where(one_m_a2 > 0.0, one_m_a2 * lax.rsqrt(one_m_a2), 0.0)
    u = root * (gi * xc)

    rowi = _tile_row_index(tb, wd)
    s = 1
    while s < SUBLANES:
        ok = rowi >= s
        u = u + jnp.where(ok, a * _tile_roll(u, s), 0.0)
        a = a * jnp.where(ok, _tile_roll(a, s), 1.0)
        s *= 2
    a_ref[...] = a
    u_ref[...] = u

    def body(i, hp):
        rows = pl.ds(pl.multiple_of(i * SUBLANES, SUBLANES), SUBLANES)
        h = u_ref[rows, :] + a_ref[rows, :] * hp
        h_ref[rows, :] = h
        return jnp.broadcast_to(h[SUBLANES - 1:SUBLANES, :], (SUBLANES, wd))

    hp = lax.fori_loop(0, tb // SUBLANES, body, hcar_ref[...], unroll=4)
    hcar_ref[...] = hp
    y_ref[...] = (h_ref[...] * _silu(gate_ref[...])).astype(y_ref.dtype)

    @pl.when(t == pl.num_programs(2) - 1)
    def _():
        hout_ref[...] = hp[0:1, :]
        cout_ref[...] = x[tb - (ncw - 1):tb, :]


def _lru(proj, row0, nb, t_len, conv_state, h0, conv_w, conv_b, wa, ba, wx, bx, lam, col_x, col_g, y_prev):
    nblk, bd, _ = wa.shape
    w = nblk * bd
    ncw = conv_w.shape[0]
    bps = nblk if t_len <= LA_CHUNK else _pick(nblk, (4, 2, 1))
    bw = bps * bd
    tb = _pick(t_len, (1024, 512, 256, 128, 64, 32))
    nt = t_len // tb
    rb0 = row0 // tb

    def col(c0):
        return pl.BlockSpec((tb, bw), lambda b, h, t: (rb0 + b * nt + t, c0 // bw + h))

    vec = pl.BlockSpec((1, bw), lambda b, h, t: (0, h))
    blk = pl.BlockSpec((bps, bd, bd), lambda b, h, t: (h, 0, 0))
    hspec = pl.BlockSpec((None, 1, bw), lambda b, h, t: (b, 0, h))
    cspec = pl.BlockSpec((None, ncw - 1, bw), lambda b, h, t: (b, 0, h))
    args = [proj, proj, conv_state, h0.reshape(nb, 1, w), conv_w, conv_b.reshape(1, w),
            wa, ba.reshape(1, w), wx, bx.reshape(1, w), lam.reshape(1, w)]
    y, h_new, c_new = pl.pallas_call(
        _lru_kernel if y_prev is None else _without_ref(_lru_kernel, len(args)),
        grid=(nb, nblk // bps, nt),
        in_specs=[col(col_x), col(col_g), cspec, hspec,
                  pl.BlockSpec((ncw, bw), lambda b, h, t: (0, h)), vec,
                  blk, vec, blk, vec, vec] + _alias_spec(y_prev),
        out_specs=[pl.BlockSpec((tb, bw), lambda b, h, t: (rb0 + b * nt + t, h)), hspec, cspec],
        out_shape=[jax.ShapeDtypeStruct((proj.shape[0], w), BF16),
                   jax.ShapeDtypeStruct((nb, 1, w), F32),
                   jax.ShapeDtypeStruct((nb, ncw - 1, w), F32)],
        scratch_shapes=[pltpu.VMEM((SUBLANES, bw), F32), pltpu.VMEM((SUBLANES, bw), F32),
                        pltpu.VMEM((tb, bw), F32), pltpu.VMEM((tb, bw), F32),
                        pltpu.VMEM((tb, bw), F32)],
        input_output_aliases=_alias_map(y_prev, len(args)),
        compiler_params=pltpu.CompilerParams(
            dimension_semantics=_ARB3,
            vmem_limit_bytes=_vmem_limit(tb * bw * (2 * (4 + 4 + 2) + 3 * 4 + 12 * 4))),
        name="rglru",
    )(*args, *_alias_arg(y_prev))
    return y, h_new.reshape(nb, w), c_new


def _merge_kernel(ya_ref, yb_ref, yc_ref, w_ref, m0_ref, m1_ref, m2_ref, o_ref):
    acc = _sigmoid(m0_ref[...]) * _dot(ya_ref[...], w_ref[0].astype(BF16))
    acc = acc + _sigmoid(m1_ref[...]) * _dot(yb_ref[...], w_ref[1].astype(BF16))
    acc = acc + _sigmoid(m2_ref[...]) * _dot(yc_ref[...], w_ref[2].astype(BF16))
    o_ref[...] = acc.astype(o_ref.dtype)


def _merge(ya, yb, yc, wbr, layer, proj, col_m):
    m, bw = ya.shape
    _, nbr, _, d = wbr.shape
    tm = _pick(m, (1408, 1056, 1024, 512, 320, 256, 128, 64, 32, 16))
    tn = next(c for c in (256, 128) if d % c == 0 and col_m % c == 0)
    yspec = pl.BlockSpec((tm, bw), lambda i, j: (i, 0), pipeline_mode=pl.Buffered(1))

    def mg(n):
        return pl.BlockSpec((tm, tn), lambda i, j: (i, (col_m + n * d) // tn + j))

    need = (3 * tm * bw * 2 + 2 * nbr * bw * tn * 4 + nbr * bw * tn * 2
            + 2 * 3 * tm * tn * 4 + 6 * tm * tn * 4)
    return pl.pallas_call(
        _merge_kernel,
        grid=(m // tm, d // tn),
        in_specs=[yspec, yspec, yspec,
                  pl.BlockSpec((None, nbr, bw, tn), lambda i, j: (layer, 0, 0, j)),
                  mg(0), mg(1), mg(2)],
        out_specs=pl.BlockSpec((tm, tn), lambda i, j: (i, j)),
        out_shape=jax.ShapeDtypeStruct((m, d), BF16),
        compiler_params=pltpu.CompilerParams(
            dimension_semantics=_ARB2, vmem_limit_bytes=_vmem_limit(need)),
        name="merge",
    )(ya, yb, yc, wbr, proj, proj, proj)


def kernel(x_prompt, x_sample, state_hgrn, state_lru_h, state_lru_conv, state_gla, norm_pre, norm_post, w_in, hg_lb_logits, hg_norm, lru_conv_w, lru_conv_b, lru_wa, lru_ba, lru_wx, lru_bx, lru_lambda, gla_w2, gla_b2, gla_norm, w_branch, w_out):
    bp, tp, d = x_prompt.shape
    bs, ts, _ = x_sample.shape
    depth, _, in_cols = w_in.shape
    _, _, hg_h, hg_dk, hg_dv = state_hgrn.shape
    hg_w = hg_h * hg_dk
    lru_w = state_lru_h.shape[-1]
    _, _, gl_h, gl_dk, gl_dv = state_gla.shape
    gl_kw, gl_vw = gl_h * gl_dk, gl_h * gl_dv
    rank = gla_w2.shape[1]
    dt = x_prompt.dtype

    c_hg = 0
    c_lx = 4 * hg_w
    c_lg = c_lx + lru_w
    c_q = c_lg + lru_w
    c_k = c_q + gl_kw
    c_v = c_k + gl_kw
    c_lr = c_v + gl_vw
    c_cg = c_lr
    c_m = c_cg + gl_vw
    n_main = in_cols - rank
    assert c_m + 3 * d == n_main

    lb_sm = jax.nn.softmax(hg_lb_logits.astype(F32), axis=0)
    lb_all = jnp.cumsum(lb_sm, axis=0) - lb_sm[0:1]
    w_in_t = jnp.transpose(w_in, (0, 2, 1))

    mp, ms = bp * tp, bs * ts
    xs = (x_prompt.reshape(mp, d), x_sample.reshape(ms, d))

    zeros = lambda shape: jnp.zeros(shape, dt)
    outs = {k: [] for k in ("hg_p", "hg_s", "lh_p", "lh_s", "lc_p", "lc_s", "gl_p", "gl_s")}
    z = _prenorm(xs, norm_pre[0])
    for l in range(depth):
        proj, lr = _matmul_t(z, w_in_t, l, c_lr, rank, "in_proj")

        groups = (
            ("p", 0, bp, tp, zeros((bp,) + state_hgrn.shape[2:]), zeros((bp, lru_w)),
             zeros((bp,) + state_lru_conv.shape[2:]), zeros((bp,) + state_gla.shape[2:])),
            ("s", mp, bs, ts, state_hgrn[l], state_lru_h[l], state_lru_conv[l], state_gla[l]),
        )
        ya = jnp.zeros((mp + ms, hg_w), BF16)
        yb = jnp.zeros((mp + ms, lru_w), BF16)
        yc = jnp.zeros((mp + ms, gl_vw), BF16)
        for tag, row0, nb, t_len, s_hg, s_lh, s_lc, s_gl in groups:
            ya, s = _hgrn(proj, row0, nb, t_len, lb_all[l], hg_norm[l], s_hg, c_hg, ya)
            outs["hg_" + tag].append(s)
            yb, hn, cn = _lru(proj, row0, nb, t_len, s_lc, s_lh, lru_conv_w[l], lru_conv_b[l],
                              lru_wa[l], lru_ba[l], lru_wx[l], lru_bx[l], lru_lambda[l], c_lx, c_lg, yb)
            outs["lh_" + tag].append(hn); outs["lc_" + tag].append(cn)
            yc, s = _gla(proj, lr, row0, nb, t_len, gla_w2[l], gla_b2[l], gla_norm[l], s_gl,
                         c_q, c_k, c_v, c_cg, yc)
            outs["gl_" + tag].append(s)

        merged = _merge(ya, yb, yc, w_branch, l, proj, c_m)
        out = _matmul(merged, w_out, l, "out_proj")
        if l < depth - 1:
            x, z = _postnorm(xs, out, norm_post[l], mp, ms, norm_pre[l + 1])
            xs = (x,)
        else:
            xs = _postnorm(xs, out, norm_post[l], mp, ms, None)

    yp = xs[0].reshape(bp, tp, d)
    ys = xs[1].reshape(bs, ts, d)
    st = lambda k: jnp.stack(outs[k])
    return (yp, ys, st("hg_p"), st("hg_s"), st("lh_p"), st("lh_s"),
            st("lc_p"), st("lc_s"), st("gl_p"), st("gl_s"))
```
